```python
import math
import jax
import jax.numpy as jnp
from jax import lax
import numpy as np


D_MODEL = 2048
BATCH = 4
SEQ = 2048
DEPTH = 4
DEC_BATCH = 2
DEC_SEQ = 4096
PAST_LEN = 128

N_MIXERS = 2
N_HGRN = (DEPTH + 1) // 2
N_ATTN = DEPTH // 2
HG_HEADS = 16
HG_DK = D_MODEL // HG_HEADS
HG_DV = D_MODEL // HG_HEADS
HG_CHUNK = 64
DA_HEADS = 8
DA_DH = D_MODEL // DA_HEADS // 2
DA_DV = 2 * DA_DH
ROPE_DIM = DA_DH // 4
ROPE_THETA = 500000.0
Q_BLOCK = 128
D_FF = 5632
CONV_W = 3
EPS = 1e-6
F_FLOOR = 1e-30

kernel_name = 'hybrid_hgrn2_diffattn_convffn_encoder'


def rmsnorm(x, g):
    xf = x.astype(jnp.float32)
    y = xf * lax.rsqrt(jnp.mean(xf * xf, axis=-1, keepdims=True) + EPS)
    return (y * g.astype(jnp.float32)).astype(x.dtype)


def gla_chunk_scan(q, k, v, logf):
    B, L, H, DK = q.shape
    DV = v.shape[-1]
    n = L // HG_CHUNK

    def to_chunks(t):
        return t.reshape(B, n, HG_CHUNK, H, t.shape[-1]).transpose(1, 0, 3, 2, 4)

    qc, kc, vc, gc = to_chunks(q), to_chunks(k), to_chunks(v), to_chunks(logf)
    mask = jnp.tril(jnp.ones((HG_CHUNK, HG_CHUNK), dtype=bool))[:, :, None]

    def step(S, inp):
        qi, ki, vi, gi = inp
        b = jnp.cumsum(gi, axis=-2)
        diff = b[..., :, None, :] - b[..., None, :, :]
        decay = jnp.where(mask, jnp.exp(jnp.where(mask, diff, 0.0)), 0.0)
        scores = jnp.einsum('bhtd,bhsd,bhtsd->bhts', qi, ki, decay)
        o = jnp.einsum('bhts,bhsv->bhtv', scores, vi) + jnp.einsum('bhtd,bhdv->bhtv', qi * jnp.exp(b), S)
        b_last = b[..., -1:, :]
        S = jnp.exp(b_last[..., 0, :])[..., None] * S + jnp.einsum('bhsd,bhsv->bhdv', ki * jnp.exp(b_last - b), vi)
        return S, o

    S0 = jnp.zeros((B, H, DK, DV), jnp.float32)
    _, o = lax.scan(step, S0, (qc, kc, vc, gc))
    return o.transpose(1, 0, 3, 2, 4).reshape(B, L, H, DV)


def hgrn2_mixer(h, w_in, w_out, norm_g, lb):
    B, L, _ = h.shape
    proj = (h @ w_in).astype(jnp.float32).reshape(B, L, 5, HG_HEADS, HG_DK)
    q = jax.nn.silu(proj[:, :, 0])
    z_fw = proj[:, :, 1]
    z_bw = proj[:, :, 2]
    v = proj[:, :, 3]
    gate = proj[:, :, 4]
    lbh = lb.astype(jnp.float32).reshape(HG_HEADS, HG_DK)

    def gates(z):
        f = lbh + (1.0 - lbh) * jax.nn.sigmoid(z)
        logf = jnp.log(jnp.maximum(f, F_FLOOR))
        k = (1.0 - lbh) * jax.nn.sigmoid(-z)
        return logf, k

    logf_fw, k_fw = gates(z_fw)
    logf_bw, k_bw = gates(z_bw)
    o_fw = gla_chunk_scan(q, k_fw, v, logf_fw)
    flip = lambda t: jnp.flip(t, axis=1)
    o_bw = flip(gla_chunk_scan(flip(q), flip(k_bw), flip(v), flip(logf_bw)))
    o = rmsnorm(o_fw + o_bw, norm_g) * jax.nn.silu(gate)
    return o.reshape(B, L, D_MODEL).astype(h.dtype) @ w_out


def rope_tables(L):
    inv = 1.0 / (ROPE_THETA ** (jnp.arange(0, ROPE_DIM, 2, dtype=jnp.float32) / ROPE_DIM))
    ang = jnp.arange(L, dtype=jnp.float32)[:, None] * inv[None, :]
    return jnp.cos(ang), jnp.sin(ang)


def apply_partial_rope(x, cos, sin):
    xf = x.astype(jnp.float32)
    half = ROPE_DIM // 2
    x1 = xf[..., :half]
    x2 = xf[..., half:ROPE_DIM]
    c = cos[None, :, None, :]
    s = sin[None, :, None, :]
    rot = jnp.concatenate([x1 * c - x2 * s, x2 * c + x1 * s], axis=-1)
    return jnp.concatenate([rot, xf[..., ROPE_DIM:]], axis=-1).astype(x.dtype)


def diff_attn_mixer(h, w_qkv, w_out, lam_params, subln_g, lambda_init, cos, sin):
    B, L, _ = h.shape
    qkv = h @ w_qkv
    q = qkv[..., :D_MODEL].reshape(B, L, 2 * DA_HEADS, DA_DH)
    k = qkv[..., D_MODEL:2 * D_MODEL].reshape(B, L, 2 * DA_HEADS, DA_DH)
    v = qkv[..., 2 * D_MODEL:].reshape(B, L, DA_HEADS, DA_DV)
    q = apply_partial_rope(q, cos, sin) * (DA_DH ** -0.5)
    k = apply_partial_rope(k, cos, sin)
    lp = lam_params.astype(jnp.float32)
    lam = jnp.exp(jnp.sum(lp[0] * lp[1])) - jnp.exp(jnp.sum(lp[2] * lp[3])) + lambda_init
    nb = L // Q_BLOCK
    qb = q.reshape(B, nb, Q_BLOCK, 2 * DA_HEADS, DA_DH).transpose(1, 0, 2, 3, 4)

    def block(qi):
        s = jnp.einsum('bqhd,bkhd->bhqk', qi, k, preferred_element_type=jnp.float32)
        p = jax.nn.softmax(s, axis=-1).reshape(B, DA_HEADS, 2, Q_BLOCK, L)
        a = p[:, :, 0] - lam * p[:, :, 1]
        return jnp.einsum('bhqk,bkhv->bqhv', a.astype(v.dtype), v)

    o = lax.map(block, qb)
    o = o.transpose(1, 0, 2, 3, 4).reshape(B, L, DA_HEADS, DA_DV)
    o = rmsnorm(o, subln_g) * (1.0 - lambda_init)
    return o.reshape(B, L, D_MODEL).astype(h.dtype) @ w_out


def conv_ffn(h, w_up, conv_w, conv_b, w_down):
    u = h @ w_up
    up = jnp.pad(u, ((0, 0), (1, 1), (0, 0)))
    u = up[:, :-2] * conv_w[0] + up[:, 1:-1] * conv_w[1] + up[:, 2:] * conv_w[2] + conv_b
    gate = u[..., :D_FF]
    val = u[..., D_FF:]
    return (jax.nn.gelu(gate, approximate=True) * val) @ w_down


def trunk(x, pre_mix_g, post_mix_g, pre_ffn_g, post_ffn_g, hg_w_in, hg_w_out, hg_norm_g, hg_lower_bounds,
          da_w_qkv, da_w_out, da_lambda, da_subln_g, ffn_w_up, ffn_conv_w, ffn_conv_b, ffn_w_down):
    L = x.shape[1]
    cos, sin = rope_tables(L)
    sm = jax.nn.softmax(hg_lower_bounds.astype(jnp.float32), axis=0)
    lbs = jnp.cumsum(sm, axis=0) - sm[0:1]
    for i in range(DEPTH):
        j = i // N_MIXERS
        h = rmsnorm(x, pre_mix_g[i])
        if i % N_MIXERS == 0:
            m = hgrn2_mixer(h, hg_w_in[j], hg_w_out[j], hg_norm_g[j], lbs[i])
        else:
            lambda_init = 0.8 - 0.6 * math.exp(-0.3 * i)
            m = diff_attn_mixer(h, da_w_qkv[j], da_w_out[j], da_lambda[j], da_subln_g[j], lambda_init, cos, sin)
        x = x + rmsnorm(m, post_mix_g[i])
        h = rmsnorm(x, pre_ffn_g[i])
        x = x + rmsnorm(conv_ffn(h, ffn_w_up[i], ffn_conv_w[i], ffn_conv_b[i], ffn_w_down[i]), post_ffn_g[i])
    return x


def setup_inputs(seed: int = 0) -> dict:
    key = jax.random.key(seed)
    ks = jax.random.split(key, 20)
    nrm = lambda k, shape, s: jax.random.normal(k, shape, jnp.float32) * s
    gain = lambda k, shape: 1.0 + 0.01 * jax.random.normal(k, shape, jnp.float32)
    return {
        'x_prompt': nrm(ks[0], (BATCH, SEQ, D_MODEL), 1.0),
        'x_sample': nrm(ks[1], (DEC_BATCH, DEC_SEQ, D_MODEL), 1.0),
        'pre_mix_g': gain(ks[2], (DEPTH, D_MODEL)),
        'post_mix_g': gain(ks[3], (DEPTH, D_MODEL)),
        'pre_ffn_g': gain(ks[4], (DEPTH, D_MODEL)),
        'post_ffn_g': gain(ks[5], (DEPTH, D_MODEL)),
        'hg_w_in': nrm(ks[6], (N_HGRN, D_MODEL, 5 * D_MODEL), D_MODEL ** -0.5),
        'hg_w_out': nrm(ks[7], (N_HGRN, D_MODEL, D_MODEL), D_MODEL ** -0.5),
        'hg_norm_g': gain(ks[8], (N_HGRN, HG_DV)),
        'hg_lower_bounds': nrm(ks[9], (DEPTH, D_MODEL), 0.1),
        'da_w_qkv': nrm(ks[10], (N_ATTN, D_MODEL, 3 * D_MODEL), D_MODEL ** -0.5),
        'da_w_out': nrm(ks[11], (N_ATTN, D_MODEL, D_MODEL), D_MODEL ** -0.5),
        'da_lambda': nrm(ks[12], (N_ATTN, 4, DA_DH), 0.1),
        'da_subln_g': gain(ks[13], (N_ATTN, DA_DV)),
        'ffn_w_up': nrm(ks[14], (DEPTH, D_MODEL, 2 * D_FF), D_MODEL ** -0.5),
        'ffn_conv_w': nrm(ks[15], (DEPTH, CONV_W, 2 * D_FF), CONV_W ** -0.5),
        'ffn_conv_b': nrm(ks[16], (DEPTH, 2 * D_FF), 0.01),
        'ffn_w_down': nrm(ks[17], (DEPTH, D_FF, D_MODEL), D_FF ** -0.5),
    }


def reference(x_prompt, x_sample, pre_mix_g, post_mix_g, pre_ffn_g, post_ffn_g, hg_w_in, hg_w_out, hg_norm_g,
              hg_lower_bounds, da_w_qkv, da_w_out, da_lambda, da_subln_g, ffn_w_up, ffn_conv_w, ffn_conv_b,
              ffn_w_down):
    y_prompt = trunk(x_prompt, pre_mix_g, post_mix_g, pre_ffn_g, post_ffn_g, hg_w_in, hg_w_out, hg_norm_g,
                     hg_lower_bounds, da_w_qkv, da_w_out, da_lambda, da_subln_g, ffn_w_up, ffn_conv_w, ffn_conv_b,
                     ffn_w_down)
    y_sample = trunk(x_sample, pre_mix_g, post_mix_g, pre_ffn_g, post_ffn_g, hg_w_in, hg_w_out, hg_norm_g,
                     hg_lower_bounds, da_w_qkv, da_w_out, da_lambda, da_subln_g, ffn_w_up, ffn_conv_w, ffn_conv_b,
                     ffn_w_down)
    return (y_prompt, y_sample)
```

```python
import functools
import math

import numpy as np
import jax
import jax.numpy as jnp
from jax import lax
from jax.experimental import pallas as pl
from jax.experimental.pallas import tpu as pltpu

D_MODEL = 2048
DEPTH = 4
HG_HEADS = 16
HG_D = 128
HG_CHUNK = 64
DA_HEADS = 8
DA_DH = 128
DA_DV = 256
ROPE_DIM = 32
ROPE_THETA = 500000.0
EPS = 1e-6
F_FLOOR = 1e-30

LANES = 128
SUBLANES = 8
VMEM_LIMIT_BYTES = 56 * 1024 * 1024

F32 = jnp.float32
BF16 = jnp.bfloat16

_NT = (((1,), (1,)), ((), ()))
_TN = (((0,), (0,)), ((), ()))


def _params(n_axes):
    return pltpu.CompilerParams(dimension_semantics=("arbitrary",) * n_axes,
                                vmem_limit_bytes=VMEM_LIMIT_BYTES)


def _rms(x, g):
    return x * lax.rsqrt(jnp.mean(x * x, axis=-1, keepdims=True) + EPS) * g


def _tile(n, pref):
    t = min(n, pref)
    while n % t:
        t //= 2
    return t


class Layout:
    def __init__(self, n_prompt, l_prompt, n_sample, l_sample):
        self.t_prompt = n_prompt * l_prompt
        self.l_prompt = l_prompt
        self.n_prompt = n_prompt
        self.t_sample = n_sample * l_sample
        self.l_sample = l_sample
        self.n_sample = n_sample
        self.t = self.t_prompt + self.t_sample
        self.l_min = min(l_prompt, l_sample)

    def seq_len(self, row):
        return jnp.where(row < self.t_prompt, self.l_prompt, self.l_sample)


def _norm_matmul_kernel(x_ref, g_ref, w_ref, o_ref, h_ref):
    @pl.when(pl.program_id(1) == 0)
    def _():
        h_ref[...] = _rms(x_ref[...], g_ref[...]).astype(BF16)

    o_ref[...] = jnp.dot(h_ref[...], w_ref[...], preferred_element_type=F32).astype(o_ref.dtype)


def norm_matmul(x, g, w, out_dtype, tm=1024, tn=1024):
    t, d = x.shape
    n = w.shape[1]
    tm, tn = _tile(t, tm), _tile(n, tn)
    return pl.pallas_call(
        _norm_matmul_kernel,
        grid=(t // tm, n // tn),
        in_specs=[pl.BlockSpec((tm, d), lambda i, j: (i, 0)),
                  pl.BlockSpec((1, d), lambda i, j: (0, 0)),
                  pl.BlockSpec((d, tn), lambda i, j: (0, j))],
        out_specs=pl.BlockSpec((tm, tn), lambda i, j: (i, j)),
        out_shape=jax.ShapeDtypeStruct((t, n), out_dtype),
        scratch_shapes=[pltpu.VMEM((tm, d), BF16)],
        compiler_params=_params(2),
        name="norm_matmul",
    )(x, g.reshape(1, d), w)


def _qkv_kernel(x_ref, g_ref, w_ref, rope_ref, o_ref, h_ref, *, n_q_tiles, n_rope_tiles):
    j = pl.program_id(1)

    @pl.when(j == 0)
    def _():
        h_ref[...] = _rms(x_ref[...], g_ref[...]).astype(BF16)

    acc = jnp.dot(h_ref[...], w_ref[...], preferred_element_type=F32)

    @pl.when(j < n_rope_tiles)
    def _():
        cos = rope_ref[:, 0:LANES]
        sin_lo = rope_ref[:, LANES:2 * LANES]
        sin_hi = rope_ref[:, 2 * LANES:3 * LANES]
        scale = jnp.where(j < n_q_tiles, DA_DH ** -0.5, 1.0).astype(F32)
        half = ROPE_DIM // 2
        for c in range(acc.shape[1] // LANES):
            a = acc[:, c * LANES:(c + 1) * LANES]
            r = a * cos + pltpu.roll(a, LANES - half, 1) * sin_lo + pltpu.roll(a, half, 1) * sin_hi
            o_ref[:, c * LANES:(c + 1) * LANES] = (r * scale).astype(o_ref.dtype)

    @pl.when(j >= n_rope_tiles)
    def _():
        o_ref[...] = acc.astype(o_ref.dtype)


def qkv_projection(x, g, w, rope_tab, tm=1024, tn=1024):
    t, d = x.shape
    n = w.shape[1]
    tm, tn = _tile(t, tm), _tile(D_MODEL, tn)
    kern = functools.partial(_qkv_kernel, n_q_tiles=D_MODEL // tn, n_rope_tiles=2 * D_MODEL // tn)
    return pl.pallas_call(
        kern,
        grid=(t // tm, n // tn),
        in_specs=[pl.BlockSpec((tm, d), lambda i, j: (i, 0)),
                  pl.BlockSpec((1, d), lambda i, j: (0, 0)),
                  pl.BlockSpec((d, tn), lambda i, j: (0, j)),
                  pl.BlockSpec((tm, 3 * LANES), lambda i, j: (i, 0))],
        out_specs=pl.BlockSpec((tm, tn), lambda i, j: (i, j)),
        out_shape=jax.ShapeDtypeStruct((t, n), BF16),
        scratch_shapes=[pltpu.VMEM((tm, d), BF16)],
        compiler_params=_params(2),
        name="qkv_rope",
    )(x, g.reshape(1, d), w, rope_tab)


def rope_table(lay):
    half = ROPE_DIM // 2

    def tab(length):
        inv = 1.0 / (ROPE_THETA ** (jnp.arange(0, ROPE_DIM, 2, dtype=F32) / ROPE_DIM))
        ang = jnp.arange(length, dtype=F32)[:, None] * inv[None, :]
        cos, sin = jnp.cos(ang), jnp.sin(ang)
        ones = jnp.ones((length, LANES - ROPE_DIM), F32)
        zeros = jnp.zeros((length, LANES - half), F32)
        c = jnp.concatenate([cos, cos, ones], axis=1)
        s_lo = jnp.concatenate([-sin, zeros], axis=1)
        s_hi = jnp.concatenate([jnp.zeros((length, half), F32), sin, zeros[:, :LANES - ROPE_DIM]], axis=1)
        return jnp.concatenate([c, s_lo, s_hi], axis=1)

    return jnp.concatenate([jnp.tile(tab(lay.l_prompt), (lay.n_prompt, 1)),
                            jnp.tile(tab(lay.l_sample), (lay.n_sample, 1))], axis=0)


N_LEVELS = int(math.log2(HG_CHUNK))
N_EXPO = 2 * (N_LEVELS + 1)


def _hgrn_constants(reverse):
    c = HG_CHUNK
    t = np.arange(c)[:, None]
    r = np.arange(c)[None, :]
    mats = [r <= t, r > t]
    masks = [t == r]
    m = c // 2
    while m >= 1:
        right = (t % (2 * m)) >= m
        mid = (t // (2 * m)) * 2 * m + m - 1
        mats += [right & (r > mid) & (r <= t), (~right) & (r > t) & (r <= mid)]
        masks.append((t // (2 * m) == r // (2 * m)) & right & ((r % (2 * m)) < m))
        m //= 2
    e = np.stack(mats).astype(np.float32)
    k = np.stack(masks).astype(np.float32)
    if reverse:
        e, k = e[:, ::-1, ::-1], k[:, ::-1, ::-1]
    return e.reshape(N_EXPO * c, c), k


def _hgrn_scan_kernel(q_ref, z_ref, v_ref, lb_ref, emat_ref, mask_ref, o_ref, st_ref, *, lay, rows, reverse):
    c = HG_CHUNK
    n_chunks = rows // c
    blk = pl.program_id(1)
    if reverse:
        blk = pl.num_programs(1) - 1 - blk
    row0 = blk * rows
    first_row = row0 + rows if reverse else row0

    @pl.when(first_row % lay.seq_len(row0) == 0)
    def _():
        st_ref[...] = jnp.zeros_like(st_ref)

    lb = lb_ref[...]
    one_m_lb = 1.0 - lb
    emat = emat_ref[...]
    last = 0 if reverse else c - 1

    def chunk(ci, carry):
        cc = n_chunks - 1 - ci if reverse else ci
        sl = pl.ds(pl.multiple_of(cc * c, c), c)
        z = z_ref[sl, :]
        q = q_ref[sl, :]
        v = v_ref[sl, :].astype(BF16)
        f = lb + one_m_lb * jax.nn.sigmoid(z)
        logf = jnp.log(jnp.maximum(f, F_FLOOR))
        k = one_m_lb * jax.nn.sigmoid(-z)
        q = q * jax.nn.sigmoid(q)
        hi = logf.astype(BF16)
        lo = (logf - hi.astype(F32)).astype(BF16)
        expo = (jnp.dot(emat, hi, preferred_element_type=F32)
                + jnp.dot(emat, lo, preferred_element_type=F32))
        dec = jnp.exp(expo)

        st = st_ref[...]
        q_in = (dec[0:c] * q).astype(BF16)
        k_in = (dec[c:2 * c] * k).astype(BF16)
        o = lax.dot_general(q_in, st.astype(BF16), _NT, preferred_element_type=F32)
        scores = mask_ref[0] * lax.dot_general(q.astype(BF16), k.astype(BF16), _NT,
                                               preferred_element_type=F32)
        for lvl in range(N_LEVELS):
            base = (2 + 2 * lvl) * c
            ql = (dec[base:base + c] * q).astype(BF16)
            kl = (dec[base + c:base + 2 * c] * k).astype(BF16)
            scores = scores + mask_ref[lvl + 1] * lax.dot_general(ql, kl, _NT, preferred_element_type=F32)
        o = o + jnp.dot(scores.astype(BF16), v, preferred_element_type=F32)
        o_ref[sl, :] = o
        st_ref[...] = st * dec[last:last + 1] + lax.dot_general(v, k_in, _TN, preferred_element_type=F32)
        return carry

    lax.fori_loop(0, n_chunks, chunk, 0)


def hgrn_scan(proj, lb, lay, reverse, rows=1024):
    t = proj.shape[0]
    rows = _tile(lay.l_min, rows)
    nb = t // rows
    emat, masks = _hgrn_constants(reverse)
    z_part = 2 if reverse else 1

    def rowblk(b):
        return nb - 1 - b if reverse else b

    def col(part):
        return lambda h, b: (rowblk(b), part * HG_HEADS + h)

    kern = functools.partial(_hgrn_scan_kernel, lay=lay, rows=rows, reverse=reverse)
    return pl.pallas_call(
        kern,
        grid=(HG_HEADS, nb),
        in_specs=[pl.BlockSpec((rows, HG_D), col(0)),
                  pl.BlockSpec((rows, HG_D), col(z_part)),
                  pl.BlockSpec((rows, HG_D), col(3)),
                  pl.BlockSpec((1, HG_D), lambda h, b: (0, h)),
                  pl.BlockSpec(emat.shape, lambda h, b: (0, 0)),
                  pl.BlockSpec(masks.shape, lambda h, b: (0, 0, 0))],
        out_specs=pl.BlockSpec((rows, HG_D), lambda h, b: (rowblk(b), h)),
        out_shape=jax.ShapeDtypeStruct((t, D_MODEL), F32),
        scratch_shapes=[pltpu.VMEM((HG_D, HG_D), F32)],
        compiler_params=_params(2),
        name="hgrn_scan_bwd" if reverse else "hgrn_scan_fwd",
    )(proj, proj, proj, lb.reshape(1, D_MODEL), jnp.asarray(emat, BF16), jnp.asarray(masks, F32))


def _hgrn_out_kernel(ofw_ref, obw_ref, gate_ref, ng_ref, w_ref, x_ref, pg_ref, out_ref, y_ref):
    ng = ng_ref[...]
    for h in range(HG_HEADS):
        sl = slice(h * HG_D, (h + 1) * HG_D)
        o = _rms(ofw_ref[:, sl] + obw_ref[:, sl], ng)
        gate = gate_ref[:, sl]
        y_ref[:, sl] = (o * (gate * jax.nn.sigmoid(gate))).astype(BF16)
    m = jnp.dot(y_ref[...], w_ref[...], preferred_element_type=F32)
    out_ref[...] = x_ref[...] + _rms(m, pg_ref[...])


def hgrn_out(o_fw, o_bw, proj, norm_g, w_out, x, post_g, tm=256):
    t, d = x.shape
    tm = _tile(t, tm)
    row = lambda i: (i, 0)
    const = lambda i: (0, 0)
    return pl.pallas_call(
        _hgrn_out_kernel,
        grid=(t // tm,),
        in_specs=[pl.BlockSpec((tm, d), row),
                  pl.BlockSpec((tm, d), row),
                  pl.BlockSpec((tm, d), lambda i: (i, 4)),
                  pl.BlockSpec((1, HG_D), const),
                  pl.BlockSpec((d, d), const),
                  pl.BlockSpec((tm, d), row),
                  pl.BlockSpec((1, d), const)],
        out_specs=pl.BlockSpec((tm, d), row),
        out_shape=jax.ShapeDtypeStruct((t, d), F32),
        scratch_shapes=[pltpu.VMEM((tm, d), BF16)],
        compiler_params=_params(1),
        name="hgrn_out",
    )(o_fw, o_bw, proj, norm_g.reshape(1, HG_D), w_out, x, post_g.reshape(1, d))


def _out_proj_kernel(y_ref, w_ref, x_ref, pg_ref, out_ref):
    m = jnp.dot(y_ref[...], w_ref[...], preferred_element_type=F32)
    out_ref[...] = x_ref[...] + _rms(m, pg_ref[...])


def out_proj(y, w_out, x, post_g, tm=512):
    t, d = x.shape
    tm = _tile(t, tm)
    row = lambda i: (i, 0)
    const = lambda i: (0, 0)
    return pl.pallas_call(
        _out_proj_kernel,
        grid=(t // tm,),
        in_specs=[pl.BlockSpec((tm, d), row),
                  pl.BlockSpec((d, d), const),
                  pl.BlockSpec((tm, d), row),
                  pl.BlockSpec((1, d), const)],
        out_specs=pl.BlockSpec((tm, d), row),
        out_shape=jax.ShapeDtypeStruct((t, d), F32),
        compiler_params=_params(1),
        name="attn_out",
    )(y, w_out, x, post_g.reshape(1, d))


def _diff_attn_kernel(q_ref, k_ref, v_ref, lam_ref, sg_ref, *rest, lambda_init):
    o_ref = rest[-1]
    lp = lam_ref[...]
    lam = (jnp.exp(jnp.sum(lp[0:1] * lp[1:2], axis=-1, keepdims=True))
           - jnp.exp(jnp.sum(lp[2:3] * lp[3:4], axis=-1, keepdims=True)) + lambda_init)
    v = v_ref[...]
    outs = []
    for sub in range(2):
        sl = slice(sub * DA_DH, (sub + 1) * DA_DH)
        s = lax.dot_general(q_ref[:, sl], k_ref[:, sl], _NT, preferred_element_type=F32)
        p = jnp.exp(s - jnp.max(s, axis=-1, keepdims=True))
        denom = jnp.sum(p, axis=-1, keepdims=True)
        outs.append(jnp.dot(p.astype(BF16), v, preferred_element_type=F32) / denom)
    o = outs[0] - lam * outs[1]
    o_ref[...] = (_rms(o, sg_ref[...]) * (1.0 - lambda_init)).astype(o_ref.dtype)


def diff_attention(qkv, lam_params, subln_g, lay, lambda_init):
    t = qkv.shape[0]
    out = None
    for row_off, n_seq, length in ((0, lay.n_prompt, lay.l_prompt),
                                   (lay.t_prompt, lay.n_sample, lay.l_sample)):
        tq = _tile(length, 512 if length <= 2048 else 256)
        nq = length // tq
        qoff, koff = row_off // tq, row_off // length
        in_specs = [pl.BlockSpec((tq, DA_DV), lambda b, h, i: (qoff + b * nq + i, h)),
                    pl.BlockSpec((length, DA_DV), lambda b, h, i: (koff + b, DA_HEADS + h)),
                    pl.BlockSpec((length, DA_DV), lambda b, h, i: (koff + b, 2 * DA_HEADS + h)),
                    pl.BlockSpec((4, DA_DH), lambda b, h, i: (0, 0)),
                    pl.BlockSpec((1, DA_DV), lambda b, h, i: (0, 0))]
        args = [qkv, qkv, qkv, lam_params, subln_g.reshape(1, DA_DV)]
        aliases = {}
        if out is not None:
            in_specs.append(pl.BlockSpec(memory_space=pl.ANY))
            args.append(out)
            aliases = {len(args) - 1: 0}
        out = pl.pallas_call(
            functools.partial(_diff_attn_kernel, lambda_init=lambda_init),
            grid=(n_seq, DA_HEADS, nq),
            in_specs=in_specs,
            out_specs=pl.BlockSpec((tq, DA_DV), lambda b, h, i: (qoff + b * nq + i, h)),
            out_shape=jax.ShapeDtypeStruct((t, D_MODEL), BF16),
            input_output_aliases=aliases,
            compiler_params=_params(3),
            name="diff_attn_len%d" % length,
        )(*args)
    return out


def _gelu_tanh(x):
    return 0.5 * x * (1.0 + jnp.tanh(math.sqrt(2.0 / math.pi) * (x + 0.044715 * (x * x * x))))


def _ffn_kernel(x_ref, xp_ref, xn_ref, g_ref, wg_ref, wv_ref, cwg_ref, cwv_ref, cbg_ref, cbv_ref,
                wd_ref, pg_ref, out_ref, h_ref, *, lay, tm):
    i, j = pl.program_id(0), pl.program_id(1)
    halo = SUBLANES

    @pl.when(j == 0)
    def _():
        g = g_ref[...]
        row0 = i * tm
        length = lay.seq_len(row0)
        has_prev = (row0 % length) != 0
        has_next = ((row0 + tm) % length) != 0
        h_ref[halo:halo + tm, :] = _rms(x_ref[...], g).astype(BF16)
        h_ref[0:halo, :] = jnp.where(has_prev, _rms(xp_ref[...], g), 0.0).astype(BF16)
        h_ref[halo + tm:2 * halo + tm, :] = jnp.where(has_next, _rms(xn_ref[...], g), 0.0).astype(BF16)
        out_ref[...] = jnp.zeros_like(out_ref)

    h = h_ref[...]

    def up_conv(w_ref, cw_ref, cb_ref):
        u = jnp.dot(h, w_ref[...], preferred_element_type=F32)
        cw = cw_ref[...]
        return (u[halo - 1:halo - 1 + tm] * cw[0:1] + u[halo:halo + tm] * cw[1:2]
                + u[halo + 1:halo + 1 + tm] * cw[2:3] + cb_ref[...])

    act = _gelu_tanh(up_conv(wg_ref, cwg_ref, cbg_ref)) * up_conv(wv_ref, cwv_ref, cbv_ref)
    out_ref[...] += jnp.dot(act.astype(BF16), wd_ref[...], preferred_element_type=F32)

    @pl.when(j == pl.num_programs(1) - 1)
    def _():
        out_ref[...] = x_ref[...] + _rms(out_ref[...], pg_ref[...])


def conv_ffn(x, pre_g, w_up, conv_w, conv_b, w_down, post_g, lay, tm=512, tn=512):
    t, d = x.shape
    f = w_down.shape[0]
    tm, tn = _tile(lay.l_min, tm), _tile(f, tn)
    nf = f // tn
    hb = tm // SUBLANES
    last_hb = t // SUBLANES - 1
    kern = functools.partial(_ffn_kernel, lay=lay, tm=tm)
    return pl.pallas_call(
        kern,
        grid=(t // tm, nf),
        in_specs=[pl.BlockSpec((tm, d), lambda i, j: (i, 0)),
                  pl.BlockSpec((SUBLANES, d), lambda i, j: (jnp.maximum(i * hb - 1, 0), 0)),
                  pl.BlockSpec((SUBLANES, d), lambda i, j: (jnp.minimum((i + 1) * hb, last_hb), 0)),
                  pl.BlockSpec((1, d), lambda i, j: (0, 0)),
                  pl.BlockSpec((d, tn), lambda i, j: (0, j)),
                  pl.BlockSpec((d, tn), lambda i, j: (0, nf + j)),
                  pl.BlockSpec((3, tn), lambda i, j: (0, j)),
                  pl.BlockSpec((3, tn), lambda i, j: (0, nf + j)),
                  pl.BlockSpec((1, tn), lambda i, j: (0, j)),
                  pl.BlockSpec((1, tn), lambda i, j: (0, nf + j)),
                  pl.BlockSpec((tn, d), lambda i, j: (j, 0)),
                  pl.BlockSpec((1, d), lambda i, j: (0, 0))],
        out_specs=pl.BlockSpec((tm, d), lambda i, j: (i, 0)),
        out_shape=jax.ShapeDtypeStruct((t, d), F32),
        scratch_shapes=[pltpu.VMEM((tm + 2 * SUBLANES, d), BF16)],
        compiler_params=_params(2),
        name="conv_ffn",
    )(x, x, x, pre_g.reshape(1, d), w_up, w_up, conv_w, conv_w, conv_b.reshape(1, 2 * f),
      conv_b.reshape(1, 2 * f), w_down, post_g.reshape(1, d))


def kernel(x_prompt, x_sample, pre_mix_g, post_mix_g, pre_ffn_g, post_ffn_g, hg_w_in, hg_w_out, hg_norm_g,
           hg_lower_bounds, da_w_qkv, da_w_out, da_lambda, da_subln_g, ffn_w_up, ffn_conv_w, ffn_conv_b,
           ffn_w_down):
    lay = Layout(x_prompt.shape[0], x_prompt.shape[1], x_sample.shape[0], x_sample.shape[1])
    x = jnp.concatenate([x_prompt.reshape(lay.t_prompt, D_MODEL), x_sample.reshape(lay.t_sample, D_MODEL)])
    rope_tab = rope_table(lay)
    sm = jax.nn.softmax(hg_lower_bounds.astype(F32), axis=0)
    lbs = jnp.cumsum(sm, axis=0) - sm[0:1]

    for i in range(DEPTH):
        j = i // 2
        if i % 2 == 0:
            proj = norm_matmul(x, pre_mix_g[i], hg_w_in[j].astype(BF16), F32)
            o_fw = hgrn_scan(proj, lbs[i], lay, reverse=False)
            o_bw = hgrn_scan(proj, lbs[i], lay, reverse=True)
            x = hgrn_out(o_fw, o_bw, proj, hg_norm_g[j], hg_w_out[j].astype(BF16), x, post_mix_g[i])
        else:
            lambda_init = 0.8 - 0.6 * math.exp(-0.3 * i)
            qkv = qkv_projection(x, pre_mix_g[i], da_w_qkv[j].astype(BF16), rope_tab)
            heads = diff_attention(qkv, da_lambda[j], da_subln_g[j], lay, lambda_init)
            x = out_proj(heads, da_w_out[j].astype(BF16), x, post_mix_g[i])
        x = conv_ffn(x, pre_ffn_g[i], ffn_w_up[i].astype(BF16), ffn_conv_w[i], ffn_conv_b[i],
                     ffn_w_down[i].astype(BF16), post_ffn_g[i], lay)

    y_prompt = x[:lay.t_prompt].reshape(x_prompt.shape)
    y_sample = x[lay.t_prompt:].reshape(x_sample.shape)
    return (y_prompt, y_sample)
```

```python
import functools
import math

import numpy as np
import jax
import jax.numpy as jnp
from jax import lax
from jax.experimental import pallas as pl
from jax.experimental.pallas import tpu as pltpu

D_MODEL = 2048
DEPTH = 4
HG_HEADS = 16
HG_D = 128
HG_CHUNK = 64
DA_HEADS = 8
DA_DH = 128
DA_DV = 256
ROPE_DIM = 32
ROPE_THETA = 500000.0
EPS = 1e-6
F_FLOOR = 1e-30

LANES = 128
SUBLANES = 8
VMEM_LIMIT_BYTES = 56 * 1024 * 1024

F32 = jnp.float32
BF16 = jnp.bfloat16

_NT = (((1,), (1,)), ((), ()))
_TN = (((0,), (0,)), ((), ()))


def _params(n_axes):
    return pltpu.CompilerParams(dimension_semantics=("arbitrary",) * n_axes,
                                vmem_limit_bytes=VMEM_LIMIT_BYTES)


def _rms(x, g):
    return x * lax.rsqrt(jnp.mean(x * x, axis=-1, keepdims=True) + EPS) * g


def _tile(n, pref):
    t = min(n, pref)
    while n % t:
        t //= 2
    return t


class Layout:
    def __init__(self, n_prompt, l_prompt, n_sample, l_sample):
        self.t_prompt = n_prompt * l_prompt
        self.l_prompt = l_prompt
        self.n_prompt = n_prompt
        self.t_sample = n_sample * l_sample
        self.l_sample = l_sample
        self.n_sample = n_sample
        self.t = self.t_prompt + self.t_sample
        self.l_min = min(l_prompt, l_sample)

    def seq_len(self, row):
        return jnp.where(row < self.t_prompt, self.l_prompt, self.l_sample)


def _norm_matmul_kernel(x_ref, g_ref, w_ref, o_ref, h_ref):
    @pl.when(pl.program_id(1) == 0)
    def _():
        h_ref[...] = _rms(x_ref[...], g_ref[...]).astype(BF16)

    o_ref[...] = jnp.dot(h_ref[...], w_ref[...], preferred_element_type=F32).astype(o_ref.dtype)


def norm_matmul(x, g, w, out_dtype, tm=1024, tn=1024):
    t, d = x.shape
    n = w.shape[1]
    tm, tn = _tile(t, tm), _tile(n, tn)
    return pl.pallas_call(
        _norm_matmul_kernel,
        grid=(t // tm, n // tn),
        in_specs=[pl.BlockSpec((tm, d), lambda i, j: (i, 0)),
                  pl.BlockSpec((1, d), lambda i, j: (0, 0)),
                  pl.BlockSpec((d, tn), lambda i, j: (0, j))],
        out_specs=pl.BlockSpec((tm, tn), lambda i, j: (i, j)),
        out_shape=jax.ShapeDtypeStruct((t, n), out_dtype),
        scratch_shapes=[pltpu.VMEM((tm, d), BF16)],
        compiler_params=_params(2),
        name="norm_matmul",
    )(x, g.reshape(1, d), w)


def _qkv_kernel(x_ref, g_ref, w_ref, rope_ref, o_ref, h_ref, *, n_q_tiles, n_rope_tiles):
    j = pl.program_id(1)

    @pl.when(j == 0)
    def _():
        h_ref[...] = _rms(x_ref[...], g_ref[...]).astype(BF16)

    acc = jnp.dot(h_ref[...], w_ref[...], preferred_element_type=F32)

    @pl.when(j < n_rope_tiles)
    def _():
        cos = rope_ref[:, 0:LANES]
        sin_lo = rope_ref[:, LANES:2 * LANES]
        sin_hi = rope_ref[:, 2 * LANES:3 * LANES]
        scale = jnp.where(j < n_q_tiles, DA_DH ** -0.5, 1.0).astype(F32)
        half = ROPE_DIM // 2
        for c in range(acc.shape[1] // LANES):
            a = acc[:, c * LANES:(c + 1) * LANES]
            r = a * cos + pltpu.roll(a, LANES - half, 1) * sin_lo + pltpu.roll(a, half, 1) * sin_hi
            o_ref[:, c * LANES:(c + 1) * LANES] = (r * scale).astype(o_ref.dtype)

    @pl.when(j >= n_rope_tiles)
    def _():
        o_ref[...] = acc.astype(o_ref.dtype)


def qkv_projection(x, g, w, rope_tab, tm=1024, tn=1024):
    t, d = x.shape
    n = w.shape[1]
    tm, tn = _tile(t, tm), _tile(D_MODEL, tn)
    kern = functools.partial(_qkv_kernel, n_q_tiles=D_MODEL // tn, n_rope_tiles=2 * D_MODEL // tn)
    return pl.pallas_call(
        kern,
        grid=(t // tm, n // tn),
        in_specs=[pl.BlockSpec((tm, d), lambda i, j: (i, 0)),
                  pl.BlockSpec((1, d), lambda i, j: (0, 0)),
                  pl.BlockSpec((d, tn), lambda i, j: (0, j)),
                  pl.BlockSpec((tm, 3 * LANES), lambda i, j: (i, 0))],
        out_specs=pl.BlockSpec((tm, tn), lambda i, j: (i, j)),
        out_shape=jax.ShapeDtypeStruct((t, n), BF16),
        scratch_shapes=[pltpu.VMEM((tm, d), BF16)],
        compiler_params=_params(2),
        name="qkv_rope",
    )(x, g.reshape(1, d), w, rope_tab)


def rope_table(lay):
    half = ROPE_DIM // 2

    def tab(length):
        inv = 1.0 / (ROPE_THETA ** (jnp.arange(0, ROPE_DIM, 2, dtype=F32) / ROPE_DIM))
        ang = jnp.arange(length, dtype=F32)[:, None] * inv[None, :]
        cos, sin = jnp.cos(ang), jnp.sin(ang)
        ones = jnp.ones((length, LANES - ROPE_DIM), F32)
        zeros = jnp.zeros((length, LANES - half), F32)
        c = jnp.concatenate([cos, cos, ones], axis=1)
        s_lo = jnp.concatenate([-sin, zeros], axis=1)
        s_hi = jnp.concatenate([jnp.zeros((length, half), F32), sin, zeros[:, :LANES - ROPE_DIM]], axis=1)
        return jnp.concatenate([c, s_lo, s_hi], axis=1)

    return jnp.concatenate([jnp.tile(tab(lay.l_prompt), (lay.n_prompt, 1)),
                            jnp.tile(tab(lay.l_sample), (lay.n_sample, 1))], axis=0)


SCAN_LEVELS = tuple(HG_CHUNK >> s for s in range(1, int(math.log2(HG_CHUNK)) + 1))
N_SCORE_TILES = (len(SCAN_LEVELS) + 2) // 2
LOG2_E = 1.0 / math.log(2.0)


def _hgrn_constants(reverse):
    c = HG_CHUNK
    t = np.arange(c)[:, None]
    s = np.arange(c)[None, :]
    masks = [t == s]
    for m in SCAN_LEVELS:
        masks.append((t // (2 * m) == s // (2 * m)) & ((t % (2 * m)) >= m) & ((s % (2 * m)) < m))
    masks.append(np.zeros_like(masks[0]))
    assert len(masks) == 2 * N_SCORE_TILES
    tri = (s <= t).astype(np.float32)
    masks = np.stack(masks).astype(np.float32)
    if reverse:
        tri, masks = tri[::-1, ::-1], masks[:, ::-1, ::-1]
    masks = np.concatenate([masks[0::2], masks[1::2]], axis=2)
    return tri, masks


def _mid_rows(b, m, reverse):
    c, d = b.shape
    r = m if reverse else m - 1
    if 2 * m >= SUBLANES:
        b3 = b.reshape(c // (2 * m), 2 * m, d)
        return jnp.broadcast_to(b3[:, r:r + 1, :], b3.shape).reshape(c, d)
    assert 4 * m == SUBLANES
    b3 = b.reshape(c // SUBLANES, SUBLANES, d)
    sub = lax.broadcasted_iota(jnp.int32, b3.shape, 1)
    first = jnp.broadcast_to(b3[:, r:r + 1, :], b3.shape)
    second = jnp.broadcast_to(b3[:, 2 * m + r:2 * m + r + 1, :], b3.shape)
    return jnp.where(sub < 2 * m, first, second).reshape(c, d)


def _hgrn_scan_kernel(q_ref, z_ref, v_ref, lb_ref, tri_ref, mask_ref, o_ref, st_ref, qin_ref, u_ref, dec_ref,
                      *, lay, rows, reverse, unroll_intra, unroll_inter):
    c = HG_CHUNK
    n_chunks = rows // c
    blk = pl.program_id(1)
    if reverse:
        blk = pl.num_programs(1) - 1 - blk
    row0 = blk * rows
    first_row = row0 + rows if reverse else row0

    @pl.when(first_row % lay.seq_len(row0) == 0)
    def _():
        st_ref[...] = jnp.zeros_like(st_ref)

    lb = lb_ref[...]
    one_m_lb = 1.0 - lb
    last = 0 if reverse else c - 1

    def rows_of(cc):
        return pl.ds(pl.multiple_of(cc * c, c), c)

    def gates(cc):
        sl = rows_of(cc)
        z = z_ref[sl, :]
        q = q_ref[sl, :]
        e = jnp.exp(-jnp.abs(z))
        r = 1.0 / (1.0 + e)
        er = e * r
        f = jnp.maximum(lb + one_m_lb * jnp.where(z >= 0, r, er), F_FLOOR)
        k = one_m_lb * jnp.where(z >= 0, er, r)
        q = q * jax.nn.sigmoid(q)
        logf = jnp.log(f) * LOG2_E
        p0 = logf.astype(BF16)
        r1 = logf - p0.astype(F32)
        p1 = r1.astype(BF16)
        p2 = (r1 - p1.astype(F32)).astype(BF16)
        tri = tri_ref[...]
        b = (jnp.dot(tri, p0, preferred_element_type=F32) + jnp.dot(tri, p1, preferred_element_type=F32)
             + jnp.dot(tri, p2, preferred_element_type=F32))
        return q, k, f, b

    def pair_scores(q, k, f, b):
        qb, kb = q.astype(BF16), k.astype(BF16)
        lhs, rhs = [qb], [kb]
        for m in SCAN_LEVELS:
            if m == 1:
                lhs.append((q * f).astype(BF16))
                rhs.append(kb)
            else:
                d = jnp.exp2(-jnp.abs(b - _mid_rows(b, m, reverse)))
                lhs.append((q * d).astype(BF16))
                rhs.append((k * d).astype(BF16))
        zero = jnp.zeros_like(kb)
        lhs.append(qb)
        rhs.append(zero)
        tiles = []
        for j in range(N_SCORE_TILES):
            a = jnp.concatenate([lhs[2 * j], lhs[2 * j + 1]], axis=1)
            w = jnp.concatenate([jnp.concatenate([rhs[2 * j], zero], axis=1),
                                 jnp.concatenate([zero, rhs[2 * j + 1]], axis=1)], axis=0)
            s = lax.dot_general(a, w, _NT, preferred_element_type=F32)
            tiles.append((mask_ref[j] * s).astype(BF16))
        return jnp.concatenate(tiles, axis=1)

    def outputs(cc, q, k, b, scores):
        sl = rows_of(cc)
        v = v_ref[sl, :].astype(BF16)
        b_end = b[last:last + 1]
        v_rep = jnp.concatenate([v] * (2 * N_SCORE_TILES), axis=0)
        o_ref[sl, :] = jnp.dot(scores, v_rep, preferred_element_type=F32)
        qin_ref[sl, :] = (q * jnp.exp2(b)).astype(BF16)
        k_in = (k * jnp.exp2(b_end - b)).astype(BF16)
        u_ref[cc] = lax.dot_general(v, k_in, _TN, preferred_element_type=F32)
        dec_ref[cc] = jnp.broadcast_to(jnp.exp2(b_end), (SUBLANES, HG_D))

    def intra(ccs):
        staged = [gates(cc) for cc in ccs]
        scores = [pair_scores(*s) for s in staged]
        for cc, (q, k, _, b), s in zip(ccs, staged, scores):
            outputs(cc, q, k, b, s)

    def inter(ccs):
        for cc in ccs:
            sl = rows_of(cc)
            st = st_ref[...]
            o_ref[sl, :] += lax.dot_general(qin_ref[sl, :], st.astype(BF16), _NT, preferred_element_type=F32)
            st_ref[...] = st * dec_ref[cc][0:1] + u_ref[cc]

    def over_chunks(body, group):
        def step(i, carry):
            cis = [i * group + u for u in range(group)]
            body([n_chunks - 1 - ci if reverse else ci for ci in cis])
            return carry
        lax.fori_loop(0, n_chunks // group, step, 0)

    over_chunks(intra, unroll_intra)
    over_chunks(inter, unroll_inter)


def hgrn_scan(proj, lb, lay, reverse, rows=2048, unroll_intra=8, unroll_inter=8):
    t = proj.shape[0]
    rows = _tile(lay.l_min, rows)
    nb = t // rows
    n_chunks = rows // HG_CHUNK
    unroll_intra, unroll_inter = _tile(n_chunks, unroll_intra), _tile(n_chunks, unroll_inter)
    tri, masks = _hgrn_constants(reverse)
    z_part = 2 if reverse else 1

    def rowblk(b):
        return nb - 1 - b if reverse else b

    def col(part):
        return lambda h, b: (rowblk(b), part * HG_HEADS + h)

    kern = functools.partial(_hgrn_scan_kernel, lay=lay, rows=rows, reverse=reverse,
                             unroll_intra=unroll_intra, unroll_inter=unroll_inter)
    return pl.pallas_call(
        kern,
        grid=(HG_HEADS, nb),
        in_specs=[pl.BlockSpec((rows, HG_D), col(0)),
                  pl.BlockSpec((rows, HG_D), col(z_part)),
                  pl.BlockSpec((rows, HG_D), col(3)),
                  pl.BlockSpec((1, HG_D), lambda h, b: (0, h)),
                  pl.BlockSpec(tri.shape, lambda h, b: (0, 0)),
                  pl.BlockSpec(masks.shape, lambda h, b: (0, 0, 0))],
        out_specs=pl.BlockSpec((rows, HG_D), lambda h, b: (rowblk(b), h)),
        out_shape=jax.ShapeDtypeStruct((t, D_MODEL), F32),
        scratch_shapes=[pltpu.VMEM((HG_D, HG_D), F32),
                        pltpu.VMEM((rows, HG_D), BF16),
                        pltpu.VMEM((n_chunks, HG_D, HG_D), F32),
                        pltpu.VMEM((n_chunks, SUBLANES, HG_D), F32)],
        compiler_params=_params(2),
        name="hgrn_scan_bwd" if reverse else "hgrn_scan_fwd",
    )(proj, proj, proj, lb.reshape(1, D_MODEL), jnp.asarray(tri, BF16), jnp.asarray(masks, F32))


def _hgrn_out_kernel(ofw_ref, obw_ref, gate_ref, ng_ref, w_ref, x_ref, pg_ref, out_ref, y_ref):
    ng = ng_ref[...]
    for h in range(HG_HEADS):
        sl = slice(h * HG_D, (h + 1) * HG_D)
        o = _rms(ofw_ref[:, sl] + obw_ref[:, sl], ng)
        gate = gate_ref[:, sl]
        y_ref[:, sl] = (o * (gate * jax.nn.sigmoid(gate))).astype(BF16)
    m = jnp.dot(y_ref[...], w_ref[...], preferred_element_type=F32)
    out_ref[...] = x_ref[...] + _rms(m, pg_ref[...])


def hgrn_out(o_fw, o_bw, proj, norm_g, w_out, x, post_g, tm=256):
    t, d = x.shape
    tm = _tile(t, tm)
    row = lambda i: (i, 0)
    const = lambda i: (0, 0)
    return pl.pallas_call(
        _hgrn_out_kernel,
        grid=(t // tm,),
        in_specs=[pl.BlockSpec((tm, d), row),
                  pl.BlockSpec((tm, d), row),
                  pl.BlockSpec((tm, d), lambda i: (i, 4)),
                  pl.BlockSpec((1, HG_D), const),
                  pl.BlockSpec((d, d), const),
                  pl.BlockSpec((tm, d), row),
                  pl.BlockSpec((1, d), const)],
        out_specs=pl.BlockSpec((tm, d), row),
        out_shape=jax.ShapeDtypeStruct((t, d), F32),
        scratch_shapes=[pltpu.VMEM((tm, d), BF16)],
        compiler_params=_params(1),
        name="hgrn_out",
    )(o_fw, o_bw, proj, norm_g.reshape(1, HG_D), w_out, x, post_g.reshape(1, d))


def _out_proj_kernel(y_ref, w_ref, x_ref, pg_ref, out_ref):
    m = jnp.dot(y_ref[...], w_ref[...], preferred_element_type=F32)
    out_ref[...] = x_ref[...] + _rms(m, pg_ref[...])


def out_proj(y, w_out, x, post_g, tm=512):
    t, d = x.shape
    tm = _tile(t, tm)
    row = lambda i: (i, 0)
    const = lambda i: (0, 0)
    return pl.pallas_call(
        _out_proj_kernel,
        grid=(t // tm,),
        in_specs=[pl.BlockSpec((tm, d), row),
                  pl.BlockSpec((d, d), const),
                  pl.BlockSpec((tm, d), row),
                  pl.BlockSpec((1, d), const)],
        out_specs=pl.BlockSpec((tm, d), row),
        out_shape=jax.ShapeDtypeStruct((t, d), F32),
        compiler_params=_params(1),
        name="attn_out",
    )(y, w_out, x, post_g.reshape(1, d))


def _diff_attn_kernel(q_ref, k_ref, v_ref, lam_ref, sg_ref, *rest, lambda_init):
    o_ref = rest[-1]
    lp = lam_ref[...]
    lam = (jnp.exp(jnp.sum(lp[0:1] * lp[1:2], axis=-1, keepdims=True))
           - jnp.exp(jnp.sum(lp[2:3] * lp[3:4], axis=-1, keepdims=True)) + lambda_init)
    v = v_ref[...]
    scores = [lax.dot_general(q_ref[:, sl], k_ref[:, sl], _NT, preferred_element_type=F32)
              for sl in (slice(0, DA_DH), slice(DA_DH, 2 * DA_DH))]
    outs = []
    for s in scores:
        p = jnp.exp(s - jnp.max(s, axis=-1, keepdims=True))
        denom = jnp.sum(p, axis=-1, keepdims=True)
        outs.append(jnp.dot(p.astype(BF16), v, preferred_element_type=F32) / denom)
    o = outs[0] - lam * outs[1]
    o_ref[...] = (_rms(o, sg_ref[...]) * (1.0 - lambda_init)).astype(o_ref.dtype)


def diff_attention(qkv, lam_params, subln_g, lay, lambda_init):
    t = qkv.shape[0]
    out = None
    for row_off, n_seq, length in ((0, lay.n_prompt, lay.l_prompt),
                                   (lay.t_prompt, lay.n_sample, lay.l_sample)):
        tq = _tile(length, 512 if length <= 2048 else 256)
        nq = length // tq
        qoff, koff = row_off // tq, row_off // length
        in_specs = [pl.BlockSpec((tq, DA_DV), lambda b, h, i: (qoff + b * nq + i, h)),
                    pl.BlockSpec((length, DA_DV), lambda b, h, i: (koff + b, DA_HEADS + h)),
                    pl.BlockSpec((length, DA_DV), lambda b, h, i: (koff + b, 2 * DA_HEADS + h)),
                    pl.BlockSpec((4, DA_DH), lambda b, h, i: (0, 0)),
                    pl.BlockSpec((1, DA_DV), lambda b, h, i: (0, 0))]
        args = [qkv, qkv, qkv, lam_params, subln_g.reshape(1, DA_DV)]
        aliases = {}
        if out is not None:
            in_specs.append(pl.BlockSpec(memory_space=pl.ANY))
            args.append(out)
            aliases = {len(args) - 1: 0}
        out = pl.pallas_call(
            functools.partial(_diff_attn_kernel, lambda_init=lambda_init),
            grid=(n_seq, DA_HEADS, nq),
            in_specs=in_specs,
            out_specs=pl.BlockSpec((tq, DA_DV), lambda b, h, i: (qoff + b * nq + i, h)),
            out_shape=jax.ShapeDtypeStruct((t, D_MODEL), BF16),
            input_output_aliases=aliases,
            compiler_params=_params(3),
            name="diff_attn_len%d" % length,
        )(*args)
    return out


def _gelu_tanh(x):
    return 0.5 * x * (1.0 + jnp.tanh(math.sqrt(2.0 / math.pi) * (x + 0.044715 * (x * x * x))))


def _ffn_kernel(x_ref, xp_ref, xn_ref, g_ref, wg_ref, wv_ref, cwg_ref, cwv_ref, cbg_ref, cbv_ref,
                wd_ref, pg_ref, out_ref, h_ref, *, lay, tm):
    i, j = pl.program_id(0), pl.program_id(1)
    halo = SUBLANES

    @pl.when(j == 0)
    def _():
        g = g_ref[...]
        row0 = i * tm
        length = lay.seq_len(row0)
        has_prev = (row0 % length) != 0
        has_next = ((row0 + tm) % length) != 0
        h_ref[halo:halo + tm, :] = _rms(x_ref[...], g).astype(BF16)
        h_ref[0:halo, :] = jnp.where(has_prev, _rms(xp_ref[...], g), 0.0).astype(BF16)
        h_ref[halo + tm:2 * halo + tm, :] = jnp.where(has_next, _rms(xn_ref[...], g), 0.0).astype(BF16)
        out_ref[...] = jnp.zeros_like(out_ref)

    h = h_ref[...]

    def up_conv(w_ref, cw_ref, cb_ref):
        u = jnp.dot(h, w_ref[...], preferred_element_type=F32)
        cw = cw_ref[...]
        return (u[halo - 1:halo - 1 + tm] * cw[0:1] + u[halo:halo + tm] * cw[1:2]
                + u[halo + 1:halo + 1 + tm] * cw[2:3] + cb_ref[...])

    act = _gelu_tanh(up_conv(wg_ref, cwg_ref, cbg_ref)) * up_conv(wv_ref, cwv_ref, cbv_ref)
    out_ref[...] += jnp.dot(act.astype(BF16), wd_ref[...], preferred_element_type=F32)

    @pl.when(j == pl.num_programs(1) - 1)
    def _():
        out_ref[...] = x_ref[...] + _rms(out_ref[...], pg_ref[...])


def conv_ffn(x, pre_g, w_up, conv_w, conv_b, w_down, post_g, lay, tm=1024, tn=512):
    t, d = x.shape
    f = w_down.shape[0]
    tm, tn = _tile(lay.l_min, tm), _tile(f, tn)
    nf = f // tn
    hb = tm // SUBLANES
    last_hb = t // SUBLANES - 1
    kern = functools.partial(_ffn_kernel, lay=lay, tm=tm)
    return pl.pallas_call(
        kern,
        grid=(t // tm, nf),
        in_specs=[pl.BlockSpec((tm, d), lambda i, j: (i, 0), pipeline_mode=pl.Buffered(1)),
                  pl.BlockSpec((SUBLANES, d), lambda i, j: (jnp.maximum(i * hb - 1, 0), 0)),
                  pl.BlockSpec((SUBLANES, d), lambda i, j: (jnp.minimum((i + 1) * hb, last_hb), 0)),
                  pl.BlockSpec((1, d), lambda i, j: (0, 0)),
                  pl.BlockSpec((d, tn), lambda i, j: (0, j)),
                  pl.BlockSpec((d, tn), lambda i, j: (0, nf + j)),
                  pl.BlockSpec((3, tn), lambda i, j: (0, j)),
                  pl.BlockSpec((3, tn), lambda i, j: (0, nf + j)),
                  pl.BlockSpec((1, tn), lambda i, j: (0, j)),
                  pl.BlockSpec((1, tn), lambda i, j: (0, nf + j)),
                  pl.BlockSpec((tn, d), lambda i, j: (j, 0)),
                  pl.BlockSpec((1, d), lambda i, j: (0, 0))],
        out_specs=pl.BlockSpec((tm, d), lambda i, j: (i, 0)),
        out_shape=jax.ShapeDtypeStruct((t, d), F32),
        scratch_shapes=[pltpu.VMEM((tm + 2 * SUBLANES, d), BF16)],
        compiler_params=_params(2),
        name="conv_ffn",
    )(x, x, x, pre_g.reshape(1, d), w_up, w_up, conv_w, conv_w, conv_b.reshape(1, 2 * f),
      conv_b.reshape(1, 2 * f), w_down, post_g.reshape(1, d))


def kernel(x_prompt, x_sample, pre_mix_g, post_mix_g, pre_ffn_g, post_ffn_g, hg_w_in, hg_w_out, hg_norm_g,
           hg_lower_bounds, da_w_qkv, da_w_out, da_lambda, da_subln_g, ffn_w_up, ffn_conv_w, ffn_conv_b,
           ffn_w_down):
    lay = Layout(x_prompt.shape[0], x_prompt.shape[1], x_sample.shape[0], x_sample.shape[1])
    x = jnp.concatenate([x_prompt.reshape(lay.t_prompt, D_MODEL), x_sample.reshape(lay.t_sample, D_MODEL)])
    rope_tab = rope_table(lay)
    sm = jax.nn.softmax(hg_lower_bounds.astype(F32), axis=0)
    lbs = jnp.cumsum(sm, axis=0) - sm[0:1]

    for i in range(DEPTH):
        j = i // 2
        if i % 2 == 0:
            proj = norm_matmul(x, pre_mix_g[i], hg_w_in[j].astype(BF16), F32)
            o_fw = hgrn_scan(proj, lbs[i], lay, reverse=False)
            o_bw = hgrn_scan(proj, lbs[i], lay, reverse=True)
            x = hgrn_out(o_fw, o_bw, proj, hg_norm_g[j], hg_w_out[j].astype(BF16), x, post_mix_g[i])
        else:
            lambda_init = 0.8 - 0.6 * math.exp(-0.3 * i)
            qkv = qkv_projection(x, pre_mix_g[i], da_w_qkv[j].astype(BF16), rope_tab)
            heads = diff_attention(qkv, da_lambda[j], da_subln_g[j], lay, lambda_init)
            x = out_proj(heads, da_w_out[j].astype(BF16), x, post_mix_g[i])
        x = conv_ffn(x, pre_ffn_g[i], ffn_w_up[i].astype(BF16), ffn_conv_w[i], ffn_conv_b[i],
                     ffn_w_down[i].astype(BF16), post_ffn_g[i], lay)

    y_prompt = x[:lay.t_prompt].reshape(x_prompt.shape)
    y_sample = x[lay.t_prompt:].reshape(x_sample.shape)
    return (y_prompt, y_sample)
```

```python
import functools
import math

import numpy as np
import jax
import jax.numpy as jnp
from jax import lax
from jax.experimental import pallas as pl
from jax.experimental.pallas import tpu as pltpu

D_MODEL = 2048
DEPTH = 4
HG_HEADS = 16
HG_D = 128
HG_CHUNK = 64
DA_HEADS = 8
DA_DH = 128
DA_DV = 256
ROPE_DIM = 32
ROPE_THETA = 500000.0
EPS = 1e-6
F_FLOOR = 1e-30

LANES = 128
SUBLANES = 8
VMEM_LIMIT_BYTES = 56 * 1024 * 1024

F32 = jnp.float32
BF16 = jnp.bfloat16

_NT = (((1,), (1,)), ((), ()))
_TN = (((0,), (0,)), ((), ()))


def _params(n_axes):
    return pltpu.CompilerParams(dimension_semantics=("arbitrary",) * n_axes,
                                vmem_limit_bytes=VMEM_LIMIT_BYTES)


def _rms(x, g):
    return x * lax.rsqrt(jnp.mean(x * x, axis=-1, keepdims=True) + EPS) * g


def _tile(n, pref):
    t = min(n, pref)
    while n % t:
        t //= 2
    return t


class Layout:
    def __init__(self, n_prompt, l_prompt, n_sample, l_sample):
        self.t_prompt = n_prompt * l_prompt
        self.l_prompt = l_prompt
        self.n_prompt = n_prompt
        self.t_sample = n_sample * l_sample
        self.l_sample = l_sample
        self.n_sample = n_sample
        self.t = self.t_prompt + self.t_sample
        self.l_min = min(l_prompt, l_sample)

    def seq_len(self, row):
        return jnp.where(row < self.t_prompt, self.l_prompt, self.l_sample)


def _norm_matmul_kernel(x_ref, g_ref, w_ref, o_ref, h_ref):
    @pl.when(pl.program_id(1) == 0)
    def _():
        h_ref[...] = _rms(x_ref[...], g_ref[...]).astype(BF16)

    o_ref[...] = jnp.dot(h_ref[...], w_ref[...], preferred_element_type=F32).astype(o_ref.dtype)


def norm_matmul(x, g, w, layer, out_dtype, tm=1024, tn=1024):
    t, d = x.shape
    n = w.shape[2]
    tm, tn = _tile(t, tm), _tile(n, tn)
    return pl.pallas_call(
        _norm_matmul_kernel,
        grid=(t // tm, n // tn),
        in_specs=[pl.BlockSpec((tm, d), lambda i, j: (i, 0)),
                  pl.BlockSpec((1, d), lambda i, j: (0, 0)),
                  pl.BlockSpec((None, d, tn), lambda i, j: (layer, 0, j))],
        out_specs=pl.BlockSpec((tm, tn), lambda i, j: (i, j)),
        out_shape=jax.ShapeDtypeStruct((t, n), out_dtype),
        scratch_shapes=[pltpu.VMEM((tm, d), BF16)],
        compiler_params=_params(2),
        name="norm_matmul",
    )(x, g.reshape(1, d), w)


def _qkv_kernel(x_ref, g_ref, w_ref, rope_ref, o_ref, h_ref):
    @pl.when(pl.program_id(1) == 0)
    def _():
        h_ref[...] = _rms(x_ref[...], g_ref[...]).astype(BF16)

    acc = jnp.dot(h_ref[...], w_ref[...], preferred_element_type=F32)
    cos = rope_ref[:, 0:LANES]
    sin_lo = rope_ref[:, LANES:2 * LANES]
    sin_hi = rope_ref[:, 2 * LANES:3 * LANES]
    half = ROPE_DIM // 2
    for c in range(acc.shape[1] // LANES):
        a = acc[:, c * LANES:(c + 1) * LANES]
        r = a * cos + pltpu.roll(a, LANES - half, 1) * sin_lo + pltpu.roll(a, half, 1) * sin_hi
        o_ref[:, c * LANES:(c + 1) * LANES] = r.astype(o_ref.dtype)


def qkv_projection(x, g, w, layer, rope_tab, lay, tm=1024, tn=1024):
    t, d = x.shape
    n = w.shape[2]
    tm, tn = _tile(lay.l_min, tm), _tile(D_MODEL, tn)
    tiles_per_part = D_MODEL // tn
    prompt_tiles = lay.t_prompt // tm

    def pos_block(i):
        return jnp.where(i < prompt_tiles, i % (lay.l_prompt // tm), (i - prompt_tiles) % (lay.l_sample // tm))

    return pl.pallas_call(
        _qkv_kernel,
        grid=(t // tm, n // tn),
        in_specs=[pl.BlockSpec((tm, d), lambda i, j: (i, 0)),
                  pl.BlockSpec((1, d), lambda i, j: (0, 0)),
                  pl.BlockSpec((None, d, tn), lambda i, j: (layer, 0, j)),
                  pl.BlockSpec((None, tm, 3 * LANES), lambda i, j: (j // tiles_per_part, pos_block(i), 0))],
        out_specs=pl.BlockSpec((tm, tn), lambda i, j: (i, j)),
        out_shape=jax.ShapeDtypeStruct((t, n), BF16),
        scratch_shapes=[pltpu.VMEM((tm, d), BF16)],
        compiler_params=_params(2),
        name="qkv_rope",
    )(x, g.reshape(1, d), w, rope_tab)


def rope_table(lay):
    half = ROPE_DIM // 2

    def tab(length):
        inv = 1.0 / (ROPE_THETA ** (jnp.arange(0, ROPE_DIM, 2, dtype=F32) / ROPE_DIM))
        ang = jnp.arange(length, dtype=F32)[:, None] * inv[None, :]
        cos, sin = jnp.cos(ang), jnp.sin(ang)
        ones = jnp.ones((length, LANES - ROPE_DIM), F32)
        zeros = jnp.zeros((length, LANES - half), F32)
        c = jnp.concatenate([cos, cos, ones], axis=1)
        s_lo = jnp.concatenate([-sin, zeros], axis=1)
        s_hi = jnp.concatenate([jnp.zeros((length, half), F32), sin, zeros[:, :LANES - ROPE_DIM]], axis=1)
        return jnp.concatenate([c, s_lo, s_hi], axis=1)

    l_max = max(lay.l_prompt, lay.l_sample)
    k_tab = tab(l_max)
    identity = jnp.concatenate([jnp.ones((l_max, LANES), F32), jnp.zeros((l_max, 2 * LANES), F32)], axis=1)
    return jnp.stack([k_tab * (DA_DH ** -0.5), k_tab, identity])


SCAN_LEVELS = tuple(HG_CHUNK >> s for s in range(1, int(math.log2(HG_CHUNK)) + 1))
N_SCORE_TILES = (len(SCAN_LEVELS) + 2) // 2
LOG2_E = 1.0 / math.log(2.0)


def _hgrn_constants(reverse):
    c = HG_CHUNK
    t = np.arange(c)[:, None]
    s = np.arange(c)[None, :]
    masks = [t == s]
    for m in SCAN_LEVELS:
        masks.append((t // (2 * m) == s // (2 * m)) & ((t % (2 * m)) >= m) & ((s % (2 * m)) < m))
    masks.append(np.zeros_like(masks[0]))
    assert len(masks) == 2 * N_SCORE_TILES
    tri = (s <= t).astype(np.float32)
    masks = np.stack(masks).astype(np.float32)
    if reverse:
        tri, masks = tri[::-1, ::-1], masks[:, ::-1, ::-1]
    masks = np.concatenate([masks[0::2], masks[1::2]], axis=2)
    return tri, masks


def _mid_rows(b, m, reverse):
    c, d = b.shape
    r = m if reverse else m - 1
    if 2 * m >= SUBLANES:
        b3 = b.reshape(c // (2 * m), 2 * m, d)
        return jnp.broadcast_to(b3[:, r:r + 1, :], b3.shape).reshape(c, d)
    assert 4 * m == SUBLANES
    b3 = b.reshape(c // SUBLANES, SUBLANES, d)
    sub = lax.broadcasted_iota(jnp.int32, b3.shape, 1)
    first = jnp.broadcast_to(b3[:, r:r + 1, :], b3.shape)
    second = jnp.broadcast_to(b3[:, 2 * m + r:2 * m + r + 1, :], b3.shape)
    return jnp.where(sub < 2 * m, first, second).reshape(c, d)


def _hgrn_scan_kernel(q_ref, z_ref, v_ref, lb_ref, tri_ref, mask_ref, o_ref, st_ref, qin_ref, u_ref, dec_ref,
                      *, lay, rows, reverse, unroll_intra, unroll_inter):
    c = HG_CHUNK
    n_chunks = rows // c
    blk = pl.program_id(1)
    if reverse:
        blk = pl.num_programs(1) - 1 - blk
    row0 = blk * rows
    first_row = row0 + rows if reverse else row0

    @pl.when(first_row % lay.seq_len(row0) == 0)
    def _():
        st_ref[...] = jnp.zeros_like(st_ref)

    lb = lb_ref[...]
    one_m_lb = 1.0 - lb
    last = 0 if reverse else c - 1

    def rows_of(cc):
        return pl.ds(pl.multiple_of(cc * c, c), c)

    def gates(cc):
        sl = rows_of(cc)
        z = z_ref[sl, :]
        q = q_ref[sl, :]
        e = jnp.exp(-jnp.abs(z))
        r = 1.0 / (1.0 + e)
        er = e * r
        f = jnp.maximum(lb + one_m_lb * jnp.where(z >= 0, r, er), F_FLOOR)
        k = one_m_lb * jnp.where(z >= 0, er, r)
        q = q * jax.nn.sigmoid(q)
        logf = jnp.log(f) * LOG2_E
        p0 = logf.astype(BF16)
        r1 = logf - p0.astype(F32)
        p1 = r1.astype(BF16)
        p2 = (r1 - p1.astype(F32)).astype(BF16)
        tri = tri_ref[...]
        b = (jnp.dot(tri, p0, preferred_element_type=F32) + jnp.dot(tri, p1, preferred_element_type=F32)
             + jnp.dot(tri, p2, preferred_element_type=F32))
        return q, k, f, b

    def pair_scores(q, k, f, b):
        qb, kb = q.astype(BF16), k.astype(BF16)
        lhs, rhs = [qb], [kb]
        for m in SCAN_LEVELS:
            if m == 1:
                lhs.append((q * f).astype(BF16))
                rhs.append(kb)
            else:
                d = jnp.exp2(-jnp.abs(b - _mid_rows(b, m, reverse)))
                lhs.append((q * d).astype(BF16))
                rhs.append((k * d).astype(BF16))
        zero = jnp.zeros_like(kb)
        lhs.append(qb)
        rhs.append(zero)
        total = None
        for j in range(N_SCORE_TILES):
            a = jnp.concatenate([lhs[2 * j], lhs[2 * j + 1]], axis=1)
            w = jnp.concatenate([jnp.concatenate([rhs[2 * j], zero], axis=1),
                                 jnp.concatenate([zero, rhs[2 * j + 1]], axis=1)], axis=0)
            s = mask_ref[j] * lax.dot_general(a, w, _NT, preferred_element_type=F32)
            total = s if total is None else total + s
        return total.astype(BF16)

    def outputs(cc, q, k, b, scores):
        sl = rows_of(cc)
        v = v_ref[sl, :].astype(BF16)
        b_end = b[last:last + 1]
        v_rep = jnp.concatenate([v, v], axis=0)
        o_ref[sl, :] = jnp.dot(scores, v_rep, preferred_element_type=F32)
        qin_ref[sl, :] = (q * jnp.exp2(b)).astype(BF16)
        k_in = (k * jnp.exp2(b_end - b)).astype(BF16)
        u_ref[cc] = lax.dot_general(v, k_in, _TN, preferred_element_type=F32)
        dec_ref[cc] = jnp.broadcast_to(jnp.exp2(b_end), (SUBLANES, HG_D))

    def intra(ccs):
        staged = [gates(cc) for cc in ccs]
        scores = [pair_scores(*s) for s in staged]
        for cc, (q, k, _, b), s in zip(ccs, staged, scores):
            outputs(cc, q, k, b, s)

    def inter(ccs):
        for cc in ccs:
            sl = rows_of(cc)
            st = st_ref[...]
            o_ref[sl, :] += lax.dot_general(qin_ref[sl, :], st.astype(BF16), _NT, preferred_element_type=F32)
            st_ref[...] = st * dec_ref[cc][0:1] + u_ref[cc]

    def over_chunks(body, group):
        def step(i, carry):
            cis = [i * group + u for u in range(group)]
            body([n_chunks - 1 - ci if reverse else ci for ci in cis])
            return carry
        lax.fori_loop(0, n_chunks // group, step, 0)

    over_chunks(intra, unroll_intra)
    over_chunks(inter, unroll_inter)


def hgrn_scan(proj, lb, lay, reverse, rows=2048, unroll_intra=8, unroll_inter=8):
    t = proj.shape[0]
    rows = _tile(lay.l_min, rows)
    nb = t // rows
    n_chunks = rows // HG_CHUNK
    unroll_intra, unroll_inter = _tile(n_chunks, unroll_intra), _tile(n_chunks, unroll_inter)
    tri, masks = _hgrn_constants(reverse)
    z_part = 2 if reverse else 1

    def rowblk(b):
        return nb - 1 - b if reverse else b

    def col(part):
        return lambda h, b: (rowblk(b), part * HG_HEADS + h)

    kern = functools.partial(_hgrn_scan_kernel, lay=lay, rows=rows, reverse=reverse,
                             unroll_intra=unroll_intra, unroll_inter=unroll_inter)
    return pl.pallas_call(
        kern,
        grid=(HG_HEADS, nb),
        in_specs=[pl.BlockSpec((rows, HG_D), col(0)),
                  pl.BlockSpec((rows, HG_D), col(z_part)),
                  pl.BlockSpec((rows, HG_D), col(3)),
                  pl.BlockSpec((1, HG_D), lambda h, b: (0, h)),
                  pl.BlockSpec(tri.shape, lambda h, b: (0, 0)),
                  pl.BlockSpec(masks.shape, lambda h, b: (0, 0, 0))],
        out_specs=pl.BlockSpec((rows, HG_D), lambda h, b: (rowblk(b), h)),
        out_shape=jax.ShapeDtypeStruct((t, D_MODEL), F32),
        scratch_shapes=[pltpu.VMEM((HG_D, HG_D), F32),
                        pltpu.VMEM((rows, HG_D), BF16),
                        pltpu.VMEM((n_chunks, HG_D, HG_D), F32),
                        pltpu.VMEM((n_chunks, SUBLANES, HG_D), F32)],
        compiler_params=_params(2),
        name="hgrn_scan_bwd" if reverse else "hgrn_scan_fwd",
    )(proj, proj, proj, lb.reshape(1, D_MODEL), jnp.asarray(tri, BF16), jnp.asarray(masks, F32))


def _hgrn_out_kernel(ofw_ref, obw_ref, gate_ref, ng_ref, w_ref, x_ref, pg_ref, out_ref, y_ref):
    ng = ng_ref[...]
    for h in range(HG_HEADS):
        sl = slice(h * HG_D, (h + 1) * HG_D)
        o = _rms(ofw_ref[:, sl] + obw_ref[:, sl], ng)
        gate = gate_ref[:, sl]
        y_ref[:, sl] = (o * (gate * jax.nn.sigmoid(gate))).astype(BF16)
    m = jnp.dot(y_ref[...], w_ref[...], preferred_element_type=F32)
    out_ref[...] = x_ref[...] + _rms(m, pg_ref[...])


def hgrn_out(o_fw, o_bw, proj, norm_g, w_out, layer, x, post_g, tm=512):
    t, d = x.shape
    tm = _tile(t, tm)
    row = lambda i: (i, 0)
    const = lambda i: (0, 0)
    return pl.pallas_call(
        _hgrn_out_kernel,
        grid=(t // tm,),
        in_specs=[pl.BlockSpec((tm, d), row),
                  pl.BlockSpec((tm, d), row),
                  pl.BlockSpec((tm, d), lambda i: (i, 4)),
                  pl.BlockSpec((1, HG_D), const),
                  pl.BlockSpec((None, d, d), lambda i: (layer, 0, 0), pipeline_mode=pl.Buffered(1)),
                  pl.BlockSpec((tm, d), row),
                  pl.BlockSpec((1, d), const)],
        out_specs=pl.BlockSpec((tm, d), row),
        out_shape=jax.ShapeDtypeStruct((t, d), F32),
        scratch_shapes=[pltpu.VMEM((tm, d), BF16)],
        compiler_params=_params(1),
        name="hgrn_out",
    )(o_fw, o_bw, proj, norm_g.reshape(1, HG_D), w_out, x, post_g.reshape(1, d))


def _out_proj_kernel(yp_ref, ys_ref, w_ref, x_ref, pg_ref, out_ref, *, prompt_tiles):
    def project(y_ref):
        m = jnp.dot(y_ref[...], w_ref[...], preferred_element_type=F32)
        out_ref[...] = x_ref[...] + _rms(m, pg_ref[...])

    pl.when(pl.program_id(0) < prompt_tiles)(lambda: project(yp_ref))
    pl.when(pl.program_id(0) >= prompt_tiles)(lambda: project(ys_ref))


def out_proj(y_prompt, y_sample, w_out, layer, x, post_g, lay, tm=512):
    t, d = x.shape
    tm = _tile(lay.l_min, tm)
    prompt_tiles = lay.t_prompt // tm
    row = lambda i: (i, 0)
    const = lambda i: (0, 0)
    return pl.pallas_call(
        functools.partial(_out_proj_kernel, prompt_tiles=prompt_tiles),
        grid=(t // tm,),
        in_specs=[pl.BlockSpec((tm, d), lambda i: (jnp.minimum(i, prompt_tiles - 1), 0)),
                  pl.BlockSpec((tm, d), lambda i: (jnp.maximum(i - prompt_tiles, 0), 0)),
                  pl.BlockSpec((None, d, d), lambda i: (layer, 0, 0)),
                  pl.BlockSpec((tm, d), row),
                  pl.BlockSpec((1, d), const)],
        out_specs=pl.BlockSpec((tm, d), row),
        out_shape=jax.ShapeDtypeStruct((t, d), F32),
        compiler_params=_params(1),
        name="attn_out",
    )(y_prompt, y_sample, w_out, x, post_g.reshape(1, d))


ATTN_SUB_TILES = 2


def _diff_attn_kernel(q_ref, k_ref, v_ref, lam_ref, sg_ref, o_ref, *, lambda_init):
    lp = lam_ref[...]
    lam = (jnp.exp(jnp.sum(lp[0:1] * lp[1:2], axis=-1, keepdims=True))
           - jnp.exp(jnp.sum(lp[2:3] * lp[3:4], axis=-1, keepdims=True)) + lambda_init)
    v = v_ref[...]
    tq = q_ref.shape[0] // ATTN_SUB_TILES
    scores = [lax.dot_general(q_ref[t * tq:(t + 1) * tq, sl], k_ref[:, sl], _NT,
                              preferred_element_type=F32)
              for t in range(ATTN_SUB_TILES) for sl in (slice(0, DA_DH), slice(DA_DH, 2 * DA_DH))]
    outs = []
    for s in scores:
        p = jnp.exp(s - jnp.max(s, axis=-1, keepdims=True))
        denom = jnp.sum(p, axis=-1, keepdims=True)
        outs.append(jnp.dot(p.astype(BF16), v, preferred_element_type=F32) / denom)
    for t in range(ATTN_SUB_TILES):
        o = outs[2 * t] - lam * outs[2 * t + 1]
        o_ref[t * tq:(t + 1) * tq, :] = (_rms(o, sg_ref[...]) * (1.0 - lambda_init)).astype(o_ref.dtype)


def diff_attention(qkv, lam_params, subln_g, lay, lambda_init):
    outs = []
    for row_off, n_seq, length in ((0, lay.n_prompt, lay.l_prompt),
                                   (lay.t_prompt, lay.n_sample, lay.l_sample)):
        tq = ATTN_SUB_TILES * _tile(length // ATTN_SUB_TILES, 512 if length <= 2048 else 256)
        nq = length // tq
        qoff, koff = row_off // tq, row_off // length
        outs.append(pl.pallas_call(
            functools.partial(_diff_attn_kernel, lambda_init=lambda_init),
            grid=(n_seq, DA_HEADS, nq),
            in_specs=[pl.BlockSpec((tq, DA_DV), lambda b, h, i, qoff=qoff, nq=nq: (qoff + b * nq + i, h)),
                      pl.BlockSpec((length, DA_DV), lambda b, h, i, koff=koff: (koff + b, DA_HEADS + h)),
                      pl.BlockSpec((length, DA_DV), lambda b, h, i, koff=koff: (koff + b, 2 * DA_HEADS + h)),
                      pl.BlockSpec((4, DA_DH), lambda b, h, i: (0, 0)),
                      pl.BlockSpec((1, DA_DV), lambda b, h, i: (0, 0))],
            out_specs=pl.BlockSpec((tq, DA_DV), lambda b, h, i, nq=nq: (b * nq + i, h)),
            out_shape=jax.ShapeDtypeStruct((n_seq * length, D_MODEL), BF16),
            compiler_params=_params(3),
            name="diff_attn_len%d" % length,
        )(qkv, qkv, qkv, lam_params, subln_g.reshape(1, DA_DV)))
    return outs


def _gelu_tanh(x):
    return 0.5 * x * (1.0 + jnp.tanh(math.sqrt(2.0 / math.pi) * (x + 0.044715 * (x * x * x))))


def _ffn_kernel(x_ref, xp_ref, xn_ref, g_ref, wg_ref, wv_ref, cwg_ref, cwv_ref, cbg_ref, cbv_ref,
                wd_ref, pg_ref, out_ref, h_ref, *, lay, tm):
    i, j = pl.program_id(0), pl.program_id(1)
    halo = SUBLANES

    @pl.when(j == 0)
    def _():
        g = g_ref[...]
        row0 = i * tm
        length = lay.seq_len(row0)
        has_prev = (row0 % length) != 0
        has_next = ((row0 + tm) % length) != 0
        h_ref[halo:halo + tm, :] = _rms(x_ref[...], g).astype(BF16)
        h_ref[0:halo, :] = jnp.where(has_prev, _rms(xp_ref[...], g), 0.0).astype(BF16)
        h_ref[halo + tm:2 * halo + tm, :] = jnp.where(has_next, _rms(xn_ref[...], g), 0.0).astype(BF16)
        out_ref[...] = jnp.zeros_like(out_ref)

    h = h_ref[...]

    def up_conv(w_ref, cw_ref, cb_ref):
        u = jnp.dot(h, w_ref[...], preferred_element_type=F32)
        cw = cw_ref[...]
        return (u[halo - 1:halo - 1 + tm] * cw[0:1] + u[halo:halo + tm] * cw[1:2]
                + u[halo + 1:halo + 1 + tm] * cw[2:3] + cb_ref[...])

    act = _gelu_tanh(up_conv(wg_ref, cwg_ref, cbg_ref)) * up_conv(wv_ref, cwv_ref, cbv_ref)
    out_ref[...] += jnp.dot(act.astype(BF16), wd_ref[...], preferred_element_type=F32)

    @pl.when(j == pl.num_programs(1) - 1)
    def _():
        out_ref[...] = x_ref[...] + _rms(out_ref[...], pg_ref[...])


def conv_ffn(x, pre_g, w_up, conv_w, conv_b, w_down, layer, post_g, lay, tm=1024, tn=512):
    t, d = x.shape
    f = w_down.shape[1]
    tm, tn = _tile(lay.l_min, tm), _tile(f, tn)
    nf = f // tn
    hb = tm // SUBLANES
    last_hb = t // SUBLANES - 1
    kern = functools.partial(_ffn_kernel, lay=lay, tm=tm)
    return pl.pallas_call(
        kern,
        grid=(t // tm, nf),
        in_specs=[pl.BlockSpec((tm, d), lambda i, j: (i, 0), pipeline_mode=pl.Buffered(1)),
                  pl.BlockSpec((SUBLANES, d), lambda i, j: (jnp.maximum(i * hb - 1, 0), 0)),
                  pl.BlockSpec((SUBLANES, d), lambda i, j: (jnp.minimum((i + 1) * hb, last_hb), 0)),
                  pl.BlockSpec((1, d), lambda i, j: (0, 0)),
                  pl.BlockSpec((None, d, tn), lambda i, j: (layer, 0, j)),
                  pl.BlockSpec((None, d, tn), lambda i, j: (layer, 0, nf + j)),
                  pl.BlockSpec((None, 3, tn), lambda i, j: (layer, 0, j)),
                  pl.BlockSpec((None, 3, tn), lambda i, j: (layer, 0, nf + j)),
                  pl.BlockSpec((None, 1, tn), lambda i, j: (layer, 0, j)),
                  pl.BlockSpec((None, 1, tn), lambda i, j: (layer, 0, nf + j)),
                  pl.BlockSpec((None, tn, d), lambda i, j: (layer, j, 0)),
                  pl.BlockSpec((1, d), lambda i, j: (0, 0))],
        out_specs=pl.BlockSpec((tm, d), lambda i, j: (i, 0)),
        out_shape=jax.ShapeDtypeStruct((t, d), F32),
        scratch_shapes=[pltpu.VMEM((tm + 2 * SUBLANES, d), BF16)],
        compiler_params=_params(2),
        name="conv_ffn",
    )(x, x, x, pre_g.reshape(1, d), w_up, w_up, conv_w, conv_w, conv_b, conv_b, w_down, post_g.reshape(1, d))


def kernel(x_prompt, x_sample, pre_mix_g, post_mix_g, pre_ffn_g, post_ffn_g, hg_w_in, hg_w_out, hg_norm_g,
           hg_lower_bounds, da_w_qkv, da_w_out, da_lambda, da_subln_g, ffn_w_up, ffn_conv_w, ffn_conv_b,
           ffn_w_down):
    lay = Layout(x_prompt.shape[0], x_prompt.shape[1], x_sample.shape[0], x_sample.shape[1])
    x = jnp.concatenate([x_prompt.reshape(lay.t_prompt, D_MODEL), x_sample.reshape(lay.t_sample, D_MODEL)])
    rope_tab = rope_table(lay)
    sm = jax.nn.softmax(hg_lower_bounds.astype(F32), axis=0)
    lbs = jnp.cumsum(sm, axis=0) - sm[0:1]
    hg_w_in, hg_w_out, da_w_qkv, da_w_out, ffn_w_up, ffn_w_down = (
        w.astype(BF16) for w in (hg_w_in, hg_w_out, da_w_qkv, da_w_out, ffn_w_up, ffn_w_down))
    ffn_conv_b = ffn_conv_b.reshape(DEPTH, 1, -1)

    for i in range(DEPTH):
        j = i // 2
        if i % 2 == 0:
            proj = norm_matmul(x, pre_mix_g[i], hg_w_in, j, F32)
            o_fw = hgrn_scan(proj, lbs[i], lay, reverse=False)
            o_bw = hgrn_scan(proj, lbs[i], lay, reverse=True)
            x = hgrn_out(o_fw, o_bw, proj, hg_norm_g[j], hg_w_out, j, x, post_mix_g[i])
        else:
            lambda_init = 0.8 - 0.6 * math.exp(-0.3 * i)
            qkv = qkv_projection(x, pre_mix_g[i], da_w_qkv, j, rope_tab, lay)
            heads_prompt, heads_sample = diff_attention(qkv, da_lambda[j], da_subln_g[j], lay, lambda_init)
            x = out_proj(heads_prompt, heads_sample, da_w_out, j, x, post_mix_g[i], lay)
        x = conv_ffn(x, pre_ffn_g[i], ffn_w_up, ffn_conv_w, ffn_conv_b, ffn_w_down, i, post_ffn_g[i], lay)

    y_prompt = x[:lay.t_prompt].reshape(x_prompt.shape)
    y_sample = x[lay.t_prompt:].reshape(x_sample.shape)
    return (y_prompt, y_sample)
```

```python
import functools
import math

import numpy as np
import jax
import jax.numpy as jnp
from jax import lax
from jax.experimental import pallas as pl
from jax.experimental.pallas import tpu as pltpu

D_MODEL = 2048
DEPTH = 4
HG_HEADS = 16
HG_D = 128
HG_CHUNK = 64
DA_HEADS = 8
DA_DH = 128
DA_DV = 256
ROPE_DIM = 32
ROPE_THETA = 500000.0
EPS = 1e-6
F_FLOOR = 1e-30
LOG2_E = 1.0 / math.log(2.0)

LANES = 128
SUBLANES = 8
VMEM_LIMIT_BYTES = 56 * 1024 * 1024

F32 = jnp.float32
BF16 = jnp.bfloat16

_NT = (((1,), (1,)), ((), ()))
_TN = (((0,), (0,)), ((), ()))


def _params(n_axes):
    return pltpu.CompilerParams(dimension_semantics=("arbitrary",) * n_axes,
                                vmem_limit_bytes=VMEM_LIMIT_BYTES)


def _rms(x, g):
    return x * lax.rsqrt(jnp.mean(x * x, axis=-1, keepdims=True) + EPS) * g


def _tile(n, pref):
    t = min(n, pref)
    while n % t:
        t //= 2
    return t


class Layout:
    def __init__(self, n_prompt, l_prompt, n_sample, l_sample):
        self.t_prompt = n_prompt * l_prompt
        self.l_prompt = l_prompt
        self.n_prompt = n_prompt
        self.t_sample = n_sample * l_sample
        self.l_sample = l_sample
        self.n_sample = n_sample
        self.t = self.t_prompt + self.t_sample
        self.l_min = min(l_prompt, l_sample)

    def seq_len(self, row):
        return jnp.where(row < self.t_prompt, self.l_prompt, self.l_sample)


def _norm_matmul_kernel(x_ref, g_ref, w_ref, o_ref, h_ref):
    @pl.when(pl.program_id(1) == 0)
    def _():
        h_ref[...] = _rms(x_ref[...], g_ref[...]).astype(BF16)

    o_ref[...] = jnp.dot(h_ref[...], w_ref[...].astype(BF16), preferred_element_type=F32).astype(o_ref.dtype)


def norm_matmul(x, g, w, layer, out_dtype, tm=1024, tn=1024):
    t, d = x.shape
    n = w.shape[2]
    tm, tn = _tile(t, tm), _tile(n, tn)
    return pl.pallas_call(
        _norm_matmul_kernel,
        grid=(t // tm, n // tn),
        in_specs=[pl.BlockSpec((tm, d), lambda i, j: (i, 0)),
                  pl.BlockSpec((1, d), lambda i, j: (0, 0)),
                  pl.BlockSpec((None, d, tn), lambda i, j: (layer, 0, j))],
        out_specs=pl.BlockSpec((tm, tn), lambda i, j: (i, j)),
        out_shape=jax.ShapeDtypeStruct((t, n), out_dtype),
        scratch_shapes=[pltpu.VMEM((tm, d), BF16)],
        compiler_params=_params(2),
        name="norm_matmul",
    )(x, g.reshape(1, d), w)


def _qkv_kernel(x_ref, g_ref, w_ref, rope_ref, o_ref, h_ref):
    @pl.when(pl.program_id(1) == 0)
    def _():
        h_ref[...] = _rms(x_ref[...], g_ref[...]).astype(BF16)

    acc = jnp.dot(h_ref[...], w_ref[...].astype(BF16), preferred_element_type=F32)
    cos = rope_ref[:, 0:LANES]
    sin_lo = rope_ref[:, LANES:2 * LANES]
    sin_hi = rope_ref[:, 2 * LANES:3 * LANES]
    half = ROPE_DIM // 2
    for c in range(acc.shape[1] // LANES):
        a = acc[:, c * LANES:(c + 1) * LANES]
        r = a * cos + pltpu.roll(a, LANES - half, 1) * sin_lo + pltpu.roll(a, half, 1) * sin_hi
        o_ref[:, c * LANES:(c + 1) * LANES] = r.astype(o_ref.dtype)


def qkv_projection(x, g, w, layer, rope_tab, lay, tm=1024, tn=1024):
    t, d = x.shape
    n = w.shape[2]
    tm, tn = _tile(lay.l_min, tm), _tile(D_MODEL, tn)
    tiles_per_part = D_MODEL // tn
    prompt_tiles = lay.t_prompt // tm

    def pos_block(i):
        return jnp.where(i < prompt_tiles, i % (lay.l_prompt // tm), (i - prompt_tiles) % (lay.l_sample // tm))

    return pl.pallas_call(
        _qkv_kernel,
        grid=(t // tm, n // tn),
        in_specs=[pl.BlockSpec((tm, d), lambda i, j: (i, 0)),
                  pl.BlockSpec((1, d), lambda i, j: (0, 0)),
                  pl.BlockSpec((None, d, tn), lambda i, j: (layer, 0, j)),
                  pl.BlockSpec((None, tm, 3 * LANES), lambda i, j: (j // tiles_per_part, pos_block(i), 0))],
        out_specs=pl.BlockSpec((tm, tn), lambda i, j: (i, j)),
        out_shape=jax.ShapeDtypeStruct((t, n), BF16),
        scratch_shapes=[pltpu.VMEM((tm, d), BF16)],
        compiler_params=_params(2),
        name="qkv_rope",
    )(x, g.reshape(1, d), w, rope_tab)


def rope_table(lay):
    half = ROPE_DIM // 2

    def tab(length):
        inv = 1.0 / (ROPE_THETA ** (jnp.arange(0, ROPE_DIM, 2, dtype=F32) / ROPE_DIM))
        ang = jnp.arange(length, dtype=F32)[:, None] * inv[None, :]
        cos, sin = jnp.cos(ang), jnp.sin(ang)
        ones = jnp.ones((length, LANES - ROPE_DIM), F32)
        zeros = jnp.zeros((length, LANES - half), F32)
        c = jnp.concatenate([cos, cos, ones], axis=1)
        s_lo = jnp.concatenate([-sin, zeros], axis=1)
        s_hi = jnp.concatenate([jnp.zeros((length, half), F32), sin, zeros[:, :LANES - ROPE_DIM]], axis=1)
        return jnp.concatenate([c, s_lo, s_hi], axis=1)

    l_max = max(lay.l_prompt, lay.l_sample)
    k_tab = tab(l_max)
    identity = jnp.concatenate([jnp.ones((l_max, LANES), F32), jnp.zeros((l_max, 2 * LANES), F32)], axis=1)
    return jnp.stack([k_tab * (DA_DH ** -0.5 * LOG2_E), k_tab, identity])


SCAN_LEVELS = tuple(HG_CHUNK >> s for s in range(1, int(math.log2(HG_CHUNK)) + 1))
N_SCORE_TILES = (len(SCAN_LEVELS) + 2) // 2


def _hgrn_constants(reverse):
    c = HG_CHUNK
    t = np.arange(c)[:, None]
    s = np.arange(c)[None, :]
    masks = [t == s]
    for m in SCAN_LEVELS:
        masks.append((t // (2 * m) == s // (2 * m)) & ((t % (2 * m)) >= m) & ((s % (2 * m)) < m))
    masks.append(np.zeros_like(masks[0]))
    assert len(masks) == 2 * N_SCORE_TILES
    tri = (s <= t).astype(np.float32)
    masks = np.stack(masks).astype(np.float32)
    if reverse:
        tri, masks = tri[::-1, ::-1], masks[:, ::-1, ::-1]
    masks = np.concatenate([masks[0::2], masks[1::2]], axis=2)
    return tri, masks


def _mid_rows(b, m, reverse):
    c, d = b.shape
    r = m if reverse else m - 1
    if 2 * m >= SUBLANES:
        b3 = b.reshape(c // (2 * m), 2 * m, d)
        return jnp.broadcast_to(b3[:, r:r + 1, :], b3.shape).reshape(c, d)
    assert 4 * m == SUBLANES
    b3 = b.reshape(c // SUBLANES, SUBLANES, d)
    sub = lax.broadcasted_iota(jnp.int32, b3.shape, 1)
    first = jnp.broadcast_to(b3[:, r:r + 1, :], b3.shape)
    second = jnp.broadcast_to(b3[:, 2 * m + r:2 * m + r + 1, :], b3.shape)
    return jnp.where(sub < 2 * m, first, second).reshape(c, d)


def _hgrn_scan_kernel(q_ref, z_ref, v_ref, lb_ref, tri_ref, mask_ref, o_ref, st_ref, qin_ref, u_ref, dec_ref,
                      *, lay, rows, reverse, unroll_intra, unroll_inter):
    c = HG_CHUNK
    n_chunks = rows // c
    blk = pl.program_id(1)
    if reverse:
        blk = pl.num_programs(1) - 1 - blk
    row0 = blk * rows
    first_row = row0 + rows if reverse else row0

    @pl.when(first_row % lay.seq_len(row0) == 0)
    def _():
        st_ref[...] = jnp.zeros_like(st_ref)

    lb = lb_ref[...]
    one_m_lb = 1.0 - lb
    last = 0 if reverse else c - 1

    def rows_of(cc):
        return pl.ds(pl.multiple_of(cc * c, c), c)

    def gates(cc):
        sl = rows_of(cc)
        z = z_ref[sl, :]
        q = q_ref[sl, :]
        e = jnp.exp(-jnp.abs(z))
        r = 1.0 / (1.0 + e)
        er = e * r
        f = jnp.maximum(lb + one_m_lb * jnp.where(z >= 0, r, er), F_FLOOR)
        k = one_m_lb * jnp.where(z >= 0, er, r)
        q = q * jax.nn.sigmoid(q)
        logf = jnp.log(f) * LOG2_E
        p0 = logf.astype(BF16)
        r1 = logf - p0.astype(F32)
        p1 = r1.astype(BF16)
        p2 = (r1 - p1.astype(F32)).astype(BF16)
        tri = tri_ref[...]
        b = (jnp.dot(tri, p0, preferred_element_type=F32) + jnp.dot(tri, p1, preferred_element_type=F32)
             + jnp.dot(tri, p2, preferred_element_type=F32))
        return q, k, f, b

    def pair_scores(q, k, f, b):
        qb, kb = q.astype(BF16), k.astype(BF16)
        lhs, rhs = [qb], [kb]
        for m in SCAN_LEVELS:
            if m == 1:
                lhs.append((q * f).astype(BF16))
                rhs.append(kb)
            else:
                d = jnp.exp2(-jnp.abs(b - _mid_rows(b, m, reverse)))
                lhs.append((q * d).astype(BF16))
                rhs.append((k * d).astype(BF16))
        zero = jnp.zeros_like(kb)
        lhs.append(qb)
        rhs.append(zero)
        total = None
        for j in range(N_SCORE_TILES):
            a = jnp.concatenate([lhs[2 * j], lhs[2 * j + 1]], axis=1)
            w = jnp.concatenate([jnp.concatenate([rhs[2 * j], zero], axis=1),
                                 jnp.concatenate([zero, rhs[2 * j + 1]], axis=1)], axis=0)
            s = mask_ref[j] * lax.dot_general(a, w, _NT, preferred_element_type=F32)
            total = s if total is None else total + s
        return total.astype(BF16)

    def outputs(cc, q, k, b, scores):
        sl = rows_of(cc)
        v = v_ref[sl, :].astype(BF16)
        b_end = b[last:last + 1]
        v_rep = jnp.concatenate([v, v], axis=0)
        o_ref[sl, :] = jnp.dot(scores, v_rep, preferred_element_type=F32)
        qin_ref[sl, :] = (q * jnp.exp2(b)).astype(BF16)
        k_in = (k * jnp.exp2(b_end - b)).astype(BF16)
        u_ref[cc] = lax.dot_general(v, k_in, _TN, preferred_element_type=F32)
        dec_ref[cc] = jnp.broadcast_to(jnp.exp2(b_end), (SUBLANES, HG_D))

    def intra(ccs):
        staged = [gates(cc) for cc in ccs]
        scores = [pair_scores(*s) for s in staged]
        for cc, (q, k, _, b), s in zip(ccs, staged, scores):
            outputs(cc, q, k, b, s)

    def inter(ccs):
        for cc in ccs:
            sl = rows_of(cc)
            st = st_ref[...]
            o_ref[sl, :] += lax.dot_general(qin_ref[sl, :], st.astype(BF16), _NT, preferred_element_type=F32)
            st_ref[...] = st * dec_ref[cc][0:1] + u_ref[cc]

    def over_chunks(body, group):
        def step(i, carry):
            cis = [i * group + u for u in range(group)]
            body([n_chunks - 1 - ci if reverse else ci for ci in cis])
            return carry
        lax.fori_loop(0, n_chunks // group, step, 0)

    over_chunks(intra, unroll_intra)
    over_chunks(inter, unroll_inter)


def hgrn_scan(proj, lb, lay, reverse, rows=2048, unroll_intra=8, unroll_inter=32):
    t = proj.shape[0]
    rows = _tile(lay.l_min, rows)
    nb = t // rows
    n_chunks = rows // HG_CHUNK
    unroll_intra, unroll_inter = _tile(n_chunks, unroll_intra), _tile(n_chunks, unroll_inter)
    tri, masks = _hgrn_constants(reverse)
    z_part = 2 if reverse else 1

    def rowblk(b):
        return nb - 1 - b if reverse else b

    def col(part):
        return lambda h, b: (rowblk(b), part * HG_HEADS + h)

    kern = functools.partial(_hgrn_scan_kernel, lay=lay, rows=rows, reverse=reverse,
                             unroll_intra=unroll_intra, unroll_inter=unroll_inter)
    return pl.pallas_call(
        kern,
        grid=(HG_HEADS, nb),
        in_specs=[pl.BlockSpec((rows, HG_D), col(0)),
                  pl.BlockSpec((rows, HG_D), col(z_part)),
                  pl.BlockSpec((rows, HG_D), col(3)),
                  pl.BlockSpec((1, HG_D), lambda h, b: (0, h)),
                  pl.BlockSpec(tri.shape, lambda h, b: (0, 0)),
                  pl.BlockSpec(masks.shape, lambda h, b: (0, 0, 0))],
        out_specs=pl.BlockSpec((rows, HG_D), lambda h, b: (rowblk(b), h)),
        out_shape=jax.ShapeDtypeStruct((t, D_MODEL), F32),
        scratch_shapes=[pltpu.VMEM((HG_D, HG_D), F32),
                        pltpu.VMEM((rows, HG_D), BF16),
                        pltpu.VMEM((n_chunks, HG_D, HG_D), F32),
                        pltpu.VMEM((n_chunks, SUBLANES, HG_D), F32)],
        compiler_params=_params(2),
        name="hgrn_scan_bwd" if reverse else "hgrn_scan_fwd",
    )(proj, proj, proj, lb.reshape(1, D_MODEL), jnp.asarray(tri, BF16), jnp.asarray(masks, F32))


def _hgrn_out_kernel(ofw_ref, obw_ref, gate_ref, ng_ref, w_ref, x_ref, pg_ref, out_ref, y_ref):
    ng = ng_ref[...]
    for h in range(HG_HEADS):
        sl = slice(h * HG_D, (h + 1) * HG_D)
        o = _rms(ofw_ref[:, sl] + obw_ref[:, sl], ng)
        gate = gate_ref[:, sl]
        y_ref[:, sl] = (o * (gate * jax.nn.sigmoid(gate))).astype(BF16)
    m = jnp.dot(y_ref[...], w_ref[...], preferred_element_type=F32)
    out_ref[...] = x_ref[...] + _rms(m, pg_ref[...])


def hgrn_out(o_fw, o_bw, proj, norm_g, w_out, layer, x, post_g, tm=512):
    t, d = x.shape
    tm = _tile(t, tm)
    row = lambda i: (i, 0)
    const = lambda i: (0, 0)
    return pl.pallas_call(
        _hgrn_out_kernel,
        grid=(t // tm,),
        in_specs=[pl.BlockSpec((tm, d), row),
                  pl.BlockSpec((tm, d), row),
                  pl.BlockSpec((tm, d), lambda i: (i, 4)),
                  pl.BlockSpec((1, HG_D), const),
                  pl.BlockSpec((None, d, d), lambda i: (layer, 0, 0), pipeline_mode=pl.Buffered(1)),
                  pl.BlockSpec((tm, d), row),
                  pl.BlockSpec((1, d), const)],
        out_specs=pl.BlockSpec((tm, d), row),
        out_shape=jax.ShapeDtypeStruct((t, d), F32),
        scratch_shapes=[pltpu.VMEM((tm, d), BF16)],
        compiler_params=_params(1),
        name="hgrn_out",
    )(o_fw, o_bw, proj, norm_g.reshape(1, HG_D), w_out, x, post_g.reshape(1, d))


def _out_proj_kernel(yp_ref, ys_ref, w_ref, x_ref, pg_ref, out_ref, *, prompt_tiles):
    def project(y_ref):
        m = jnp.dot(y_ref[...], w_ref[...], preferred_element_type=F32)
        out_ref[...] = x_ref[...] + _rms(m, pg_ref[...])

    pl.when(pl.program_id(0) < prompt_tiles)(lambda: project(yp_ref))
    pl.when(pl.program_id(0) >= prompt_tiles)(lambda: project(ys_ref))


def out_proj(y_prompt, y_sample, w_out, layer, x, post_g, lay, tm=512):
    t, d = x.shape
    tm = _tile(lay.l_min, tm)
    prompt_tiles = lay.t_prompt // tm
    row = lambda i: (i, 0)
    const = lambda i: (0, 0)
    return pl.pallas_call(
        functools.partial(_out_proj_kernel, prompt_tiles=prompt_tiles),
        grid=(t // tm,),
        in_specs=[pl.BlockSpec((tm, d), lambda i: (jnp.minimum(i, prompt_tiles - 1), 0)),
                  pl.BlockSpec((tm, d), lambda i: (jnp.maximum(i - prompt_tiles, 0), 0)),
                  pl.BlockSpec((None, d, d), lambda i: (layer, 0, 0)),
                  pl.BlockSpec((tm, d), row),
                  pl.BlockSpec((1, d), const)],
        out_specs=pl.BlockSpec((tm, d), row),
        out_shape=jax.ShapeDtypeStruct((t, d), F32),
        compiler_params=_params(1),
        name="attn_out",
    )(y_prompt, y_sample, w_out, x, post_g.reshape(1, d))


ATTN_SUB_TILES = 2


def _diff_attn_kernel(q_ref, k_ref, v_ref, lam_ref, sg_ref, o_ref, *, lambda_init):
    lp = lam_ref[...]
    lam = (jnp.exp(jnp.sum(lp[0:1] * lp[1:2], axis=-1, keepdims=True))
           - jnp.exp(jnp.sum(lp[2:3] * lp[3:4], axis=-1, keepdims=True)) + lambda_init)
    v = v_ref[...]
    tq = q_ref.shape[0] // ATTN_SUB_TILES
    scores = [lax.dot_general(q_ref[t * tq:(t + 1) * tq, sl], k_ref[:, sl], _NT,
                              preferred_element_type=F32)
              for t in range(ATTN_SUB_TILES) for sl in (slice(0, DA_DH), slice(DA_DH, 2 * DA_DH))]
    outs = []
    for s in scores:
        p = jnp.exp2(s - jnp.max(s, axis=-1, keepdims=True))
        denom = jnp.sum(p, axis=-1, keepdims=True)
        outs.append(jnp.dot(p.astype(BF16), v, preferred_element_type=F32) / denom)
    for t in range(ATTN_SUB_TILES):
        o = outs[2 * t] - lam * outs[2 * t + 1]
        o_ref[t * tq:(t + 1) * tq, :] = (_rms(o, sg_ref[...]) * (1.0 - lambda_init)).astype(o_ref.dtype)


def diff_attention(qkv, lam_params, subln_g, lay, lambda_init):
    outs = []
    for row_off, n_seq, length in ((0, lay.n_prompt, lay.l_prompt),
                                   (lay.t_prompt, lay.n_sample, lay.l_sample)):
        tq = ATTN_SUB_TILES * _tile(length // ATTN_SUB_TILES, 512 if length <= 2048 else 256)
        nq = length // tq
        qoff, koff = row_off // tq, row_off // length
        outs.append(pl.pallas_call(
            functools.partial(_diff_attn_kernel, lambda_init=lambda_init),
            grid=(n_seq, DA_HEADS, nq),
            in_specs=[pl.BlockSpec((tq, DA_DV), lambda b, h, i, qoff=qoff, nq=nq: (qoff + b * nq + i, h)),
                      pl.BlockSpec((length, DA_DV), lambda b, h, i, koff=koff: (koff + b, DA_HEADS + h)),
                      pl.BlockSpec((length, DA_DV), lambda b, h, i, koff=koff: (koff + b, 2 * DA_HEADS + h)),
                      pl.BlockSpec((4, DA_DH), lambda b, h, i: (0, 0)),
                      pl.BlockSpec((1, DA_DV), lambda b, h, i: (0, 0))],
            out_specs=pl.BlockSpec((tq, DA_DV), lambda b, h, i, nq=nq: (b * nq + i, h)),
            out_shape=jax.ShapeDtypeStruct((n_seq * length, D_MODEL), BF16),
            compiler_params=_params(3),
            name="diff_attn_len%d" % length,
        )(qkv, qkv, qkv, lam_params, subln_g.reshape(1, DA_DV)))
    return outs


def _gelu_tanh(x):
    return 0.5 * x * (1.0 + jnp.tanh(math.sqrt(2.0 / math.pi) * (x + 0.044715 * (x * x * x))))


def _ffn_kernel(x_ref, xp_ref, xn_ref, g_ref, wg_ref, wv_ref, cwg_ref, cwv_ref, cbg_ref, cbv_ref,
                wd_ref, pg_ref, out_ref, h_ref, *, lay, tm):
    i, j = pl.program_id(0), pl.program_id(1)
    halo = SUBLANES

    @pl.when(j == 0)
    def _():
        g = g_ref[...]
        row0 = i * tm
        length = lay.seq_len(row0)
        has_prev = (row0 % length) != 0
        has_next = ((row0 + tm) % length) != 0
        h_ref[halo:halo + tm, :] = _rms(x_ref[...], g).astype(BF16)
        h_ref[0:halo, :] = jnp.where(has_prev, _rms(xp_ref[...], g), 0.0).astype(BF16)
        h_ref[halo + tm:2 * halo + tm, :] = jnp.where(has_next, _rms(xn_ref[...], g), 0.0).astype(BF16)
        out_ref[...] = jnp.zeros_like(out_ref)

    h = h_ref[...]

    def up_conv(w_ref, cw_ref, cb_ref):
        u = jnp.dot(h, w_ref[...], preferred_element_type=F32)
        cw = cw_ref[...]
        return (u[halo - 1:halo - 1 + tm] * cw[0:1] + u[halo:halo + tm] * cw[1:2]
                + u[halo + 1:halo + 1 + tm] * cw[2:3] + cb_ref[...])

    act = _gelu_tanh(up_conv(wg_ref, cwg_ref, cbg_ref)) * up_conv(wv_ref, cwv_ref, cbv_ref)
    out_ref[...] += jnp.dot(act.astype(BF16), wd_ref[...], preferred_element_type=F32)

    @pl.when(j == pl.num_programs(1) - 1)
    def _():
        out_ref[...] = x_ref[...] + _rms(out_ref[...], pg_ref[...])


def conv_ffn(x, pre_g, w_up, conv_w, conv_b, w_down, layer, post_g, lay, tm=1024, tn=512):
    t, d = x.shape
    f = w_down.shape[1]
    tm, tn = _tile(lay.l_min, tm), _tile(f, tn)
    nf = f // tn
    hb = tm // SUBLANES
    last_hb = t // SUBLANES - 1
    kern = functools.partial(_ffn_kernel, lay=lay, tm=tm)
    return pl.pallas_call(
        kern,
        grid=(t // tm, nf),
        in_specs=[pl.BlockSpec((tm, d), lambda i, j: (i, 0), pipeline_mode=pl.Buffered(1)),
                  pl.BlockSpec((SUBLANES, d), lambda i, j: (jnp.maximum(i * hb - 1, 0), 0)),
                  pl.BlockSpec((SUBLANES, d), lambda i, j: (jnp.minimum((i + 1) * hb, last_hb), 0)),
                  pl.BlockSpec((1, d), lambda i, j: (0, 0)),
                  pl.BlockSpec((None, d, tn), lambda i, j: (layer, 0, j)),
                  pl.BlockSpec((None, d, tn), lambda i, j: (layer, 0, nf + j)),
                  pl.BlockSpec((None, 3, tn), lambda i, j: (layer, 0, j)),
                  pl.BlockSpec((None, 3, tn), lambda i, j: (layer, 0, nf + j)),
                  pl.BlockSpec((None, 1, tn), lambda i, j: (layer, 0, j)),
                  pl.BlockSpec((None, 1, tn), lambda i, j: (layer, 0, nf + j)),
                  pl.BlockSpec((None, tn, d), lambda i, j: (layer, j, 0)),
                  pl.BlockSpec((1, d), lambda i, j: (0, 0))],
        out_specs=pl.BlockSpec((tm, d), lambda i, j: (i, 0)),
        out_shape=jax.ShapeDtypeStruct((t, d), F32),
        scratch_shapes=[pltpu.VMEM((tm + 2 * SUBLANES, d), BF16)],
        compiler_params=_params(2),
        name="conv_ffn",
    )(x, x, x, pre_g.reshape(1, d), w_up, w_up, conv_w, conv_w, conv_b, conv_b, w_down, post_g.reshape(1, d))


def kernel(x_prompt, x_sample, pre_mix_g, post_mix_g, pre_ffn_g, post_ffn_g, hg_w_in, hg_w_out, hg_norm_g,
           hg_lower_bounds, da_w_qkv, da_w_out, da_lambda, da_subln_g, ffn_w_up, ffn_conv_w, ffn_conv_b,
           ffn_w_down):
    lay = Layout(x_prompt.shape[0], x_prompt.shape[1], x_sample.shape[0], x_sample.shape[1])
    x = jnp.concatenate([x_prompt.reshape(lay.t_prompt, D_MODEL), x_sample.reshape(lay.t_sample, D_MODEL)])
    rope_tab = rope_table(lay)
    sm = jax.nn.softmax(hg_lower_bounds.astype(F32), axis=0)
    lbs = jnp.cumsum(sm, axis=0) - sm[0:1]
    hg_w_out, da_w_out, ffn_w_up, ffn_w_down = (
        w.astype(BF16) for w in (hg_w_out, da_w_out, ffn_w_up, ffn_w_down))
    ffn_conv_b = ffn_conv_b.reshape(DEPTH, 1, -1)

    for i in range(DEPTH):
        j = i // 2
        if i % 2 == 0:
            proj = norm_matmul(x, pre_mix_g[i], hg_w_in, j, F32)
            o_fw = hgrn_scan(proj, lbs[i], lay, reverse=False)
            o_bw = hgrn_scan(proj, lbs[i], lay, reverse=True)
            x = hgrn_out(o_fw, o_bw, proj, hg_norm_g[j], hg_w_out, j, x, post_mix_g[i])
        else:
            lambda_init = 0.8 - 0.6 * math.exp(-0.3 * i)
            qkv = qkv_projection(x, pre_mix_g[i], da_w_qkv, j, rope_tab, lay)
            heads_prompt, heads_sample = diff_attention(qkv, da_lambda[j], da_subln_g[j], lay, lambda_init)
            x = out_proj(heads_prompt, heads_sample, da_w_out, j, x, post_mix_g[i], lay)
        x = conv_ffn(x, pre_ffn_g[i], ffn_w_up, ffn_conv_w, ffn_conv_b, ffn_w_down, i, post_ffn_g[i], lay)

    y_prompt = x[:lay.t_prompt].reshape(x_prompt.shape)
    y_sample = x[lay.t_prompt:].reshape(x_sample.shape)
    return (y_prompt, y_sample)
```

```python
import functools
import math

import numpy as np
import jax
import jax.numpy as jnp
from jax import lax
from jax.experimental import pallas as pl
from jax.experimental.pallas import tpu as pltpu

D_MODEL = 2048
DEPTH = 4
HG_HEADS = 16
HG_D = 128
HG_CHUNK = 64
DA_HEADS = 8
DA_DH = 128
DA_DV = 256
ROPE_DIM = 32
ROPE_THETA = 500000.0
EPS = 1e-6
F_FLOOR = 1e-30
LOG2_E = 1.0 / math.log(2.0)

MXU_DEPTH = 256
LANES = 128
SUBLANES = 8
VMEM_LIMIT_BYTES = 58 * 1024 * 1024

F32 = jnp.float32
BF16 = jnp.bfloat16

_NT = (((1,), (1,)), ((), ()))
_TN = (((0,), (0,)), ((), ()))


def _params(n_axes):
    return pltpu.CompilerParams(dimension_semantics=("arbitrary",) * n_axes,
                                vmem_limit_bytes=VMEM_LIMIT_BYTES)


def _rms(x, g):
    return x * lax.rsqrt(jnp.mean(x * x, axis=-1, keepdims=True) + EPS) * g


def _tile(n, pref):
    t = min(n, pref)
    while n % t:
        t //= 2
    return t


class Layout:
    def __init__(self, n_prompt, l_prompt, n_sample, l_sample):
        self.t_prompt = n_prompt * l_prompt
        self.l_prompt = l_prompt
        self.n_prompt = n_prompt
        self.t_sample = n_sample * l_sample
        self.l_sample = l_sample
        self.n_sample = n_sample
        self.t = self.t_prompt + self.t_sample
        self.l_min = min(l_prompt, l_sample)

    def seq_len(self, row):
        return jnp.where(row < self.t_prompt, self.l_prompt, self.l_sample)


def _norm_matmul_kernel(x_ref, g_ref, w_ref, o_ref, h_ref):
    @pl.when(pl.program_id(1) == 0)
    def _():
        h_ref[...] = _rms(x_ref[...], g_ref[...]).astype(BF16)

    o_ref[...] = jnp.dot(h_ref[...], w_ref[...], preferred_element_type=F32).astype(o_ref.dtype)


def norm_matmul(x, g, w, layer, out_dtype, tm=1024, tn=1024):
    t, d = x.shape
    n = w.shape[2]
    tm, tn = _tile(t, tm), _tile(n, tn)
    return pl.pallas_call(
        _norm_matmul_kernel,
        grid=(t // tm, n // tn),
        in_specs=[pl.BlockSpec((tm, d), lambda i, j: (i, 0)),
                  pl.BlockSpec((1, d), lambda i, j: (0, 0)),
                  pl.BlockSpec((None, d, tn), lambda i, j: (layer, 0, j))],
        out_specs=pl.BlockSpec((tm, tn), lambda i, j: (i, j)),
        out_shape=jax.ShapeDtypeStruct((t, n), out_dtype),
        scratch_shapes=[pltpu.VMEM((tm, d), BF16)],
        compiler_params=_params(2),
        name="norm_matmul",
    )(x, g.reshape(1, d), w)


def _qkv_kernel(x_ref, g_ref, w_ref, rope_ref, o_ref, h_ref):
    @pl.when(pl.program_id(1) == 0)
    def _():
        h_ref[...] = _rms(x_ref[...], g_ref[...]).astype(BF16)

    acc = jnp.dot(h_ref[...], w_ref[...], preferred_element_type=F32)
    cos = rope_ref[:, 0:LANES]
    sin_lo = rope_ref[:, LANES:2 * LANES]
    sin_hi = rope_ref[:, 2 * LANES:3 * LANES]
    half = ROPE_DIM // 2
    for c in range(acc.shape[1] // LANES):
        a = acc[:, c * LANES:(c + 1) * LANES]
        r = a * cos + pltpu.roll(a, LANES - half, 1) * sin_lo + pltpu.roll(a, half, 1) * sin_hi
        o_ref[:, c * LANES:(c + 1) * LANES] = r.astype(o_ref.dtype)


def qkv_projection(x, g, w, layer, rope_tab, lay, tm=1024, tn=1024):
    t, d = x.shape
    n = w.shape[2]
    tm, tn = _tile(lay.l_min, tm), _tile(D_MODEL, tn)
    tiles_per_part = D_MODEL // tn
    prompt_tiles = lay.t_prompt // tm

    def pos_block(i):
        return jnp.where(i < prompt_tiles, i % (lay.l_prompt // tm), (i - prompt_tiles) % (lay.l_sample // tm))

    return pl.pallas_call(
        _qkv_kernel,
        grid=(t // tm, n // tn),
        in_specs=[pl.BlockSpec((tm, d), lambda i, j: (i, 0)),
                  pl.BlockSpec((1, d), lambda i, j: (0, 0)),
                  pl.BlockSpec((None, d, tn), lambda i, j: (layer, 0, j)),
                  pl.BlockSpec((None, tm, 3 * LANES), lambda i, j: (j // tiles_per_part, pos_block(i), 0))],
        out_specs=pl.BlockSpec((tm, tn), lambda i, j: (i, j)),
        out_shape=jax.ShapeDtypeStruct((t, n), BF16),
        scratch_shapes=[pltpu.VMEM((tm, d), BF16)],
        compiler_params=_params(2),
        name="qkv_rope",
    )(x, g.reshape(1, d), w, rope_tab)


def rope_table(lay):
    half = ROPE_DIM // 2

    def tab(length):
        inv = 1.0 / (ROPE_THETA ** (jnp.arange(0, ROPE_DIM, 2, dtype=F32) / ROPE_DIM))
        ang = jnp.arange(length, dtype=F32)[:, None] * inv[None, :]
        cos, sin = jnp.cos(ang), jnp.sin(ang)
        ones = jnp.ones((length, LANES - ROPE_DIM), F32)
        zeros = jnp.zeros((length, LANES - half), F32)
        c = jnp.concatenate([cos, cos, ones], axis=1)
        s_lo = jnp.concatenate([-sin, zeros], axis=1)
        s_hi = jnp.concatenate([jnp.zeros((length, half), F32), sin, zeros[:, :LANES - ROPE_DIM]], axis=1)
        return jnp.concatenate([c, s_lo, s_hi], axis=1)

    l_max = max(lay.l_prompt, lay.l_sample)
    k_tab = tab(l_max)
    identity = jnp.concatenate([jnp.ones((l_max, LANES), F32), jnp.zeros((l_max, 2 * LANES), F32)], axis=1)
    return jnp.stack([k_tab * (DA_DH ** -0.5 * LOG2_E), k_tab, identity])


SCAN_LEVELS = tuple(HG_CHUNK >> s for s in range(1, int(math.log2(HG_CHUNK)) + 1))
N_SCORE_TILES = (len(SCAN_LEVELS) + 2) // 2


def _hgrn_constants(reverse):
    c = HG_CHUNK
    t = np.arange(c)[:, None]
    s = np.arange(c)[None, :]
    masks = [t == s]
    for m in SCAN_LEVELS:
        masks.append((t // (2 * m) == s // (2 * m)) & ((t % (2 * m)) >= m) & ((s % (2 * m)) < m))
    masks.append(np.zeros_like(masks[0]))
    assert len(masks) == 2 * N_SCORE_TILES
    tri = (s <= t).astype(np.float32)
    masks = np.stack(masks).astype(np.float32)
    if reverse:
        tri, masks = tri[::-1, ::-1], masks[:, ::-1, ::-1]
    masks = np.concatenate([masks[0::2], masks[1::2]], axis=2)
    return tri, masks


def _mid_rows(b, m, reverse):
    c, d = b.shape
    r = m if reverse else m - 1
    if 2 * m >= SUBLANES:
        b3 = b.reshape(c // (2 * m), 2 * m, d)
        return jnp.broadcast_to(b3[:, r:r + 1, :], b3.shape).reshape(c, d)
    assert 4 * m == SUBLANES
    b3 = b.reshape(c // SUBLANES, SUBLANES, d)
    sub = lax.broadcasted_iota(jnp.int32, b3.shape, 1)
    first = jnp.broadcast_to(b3[:, r:r + 1, :], b3.shape)
    second = jnp.broadcast_to(b3[:, 2 * m + r:2 * m + r + 1, :], b3.shape)
    return jnp.where(sub < 2 * m, first, second).reshape(c, d)


def _hgrn_scan_kernel(q_ref, z_ref, v_ref, lb_ref, tri_ref, mask_ref, o_ref, st_ref, qin_ref, u_ref, dec_ref,
                      *, lay, rows, reverse, unroll_intra, unroll_inter):
    c = HG_CHUNK
    n_chunks = rows // c
    blk = pl.program_id(1)
    if reverse:
        blk = pl.num_programs(1) - 1 - blk
    row0 = blk * rows
    first_row = row0 + rows if reverse else row0

    @pl.when(first_row % lay.seq_len(row0) == 0)
    def _():
        st_ref[...] = jnp.zeros_like(st_ref)

    lb = lb_ref[...]
    one_m_lb = 1.0 - lb
    last = 0 if reverse else c - 1

    def rows_of(cc):
        return pl.ds(pl.multiple_of(cc * c, c), c)

    def gates(cc):
        sl = rows_of(cc)
        z = z_ref[sl, :]
        q = q_ref[sl, :]
        e = jnp.exp2(jnp.abs(z) * -LOG2_E)
        r = 1.0 / (1.0 + e)
        er = e * r
        f = jnp.maximum(lb + one_m_lb * jnp.where(z >= 0, r, er), F_FLOOR)
        k = one_m_lb * jnp.where(z >= 0, er, r)
        q = q / (1.0 + jnp.exp2(q * -LOG2_E))
        logf = jnp.log(f) * LOG2_E
        p0 = logf.astype(BF16)
        r1 = logf - p0.astype(F32)
        p1 = r1.astype(BF16)
        p2 = (r1 - p1.astype(F32)).astype(BF16)
        tri = tri_ref[...]
        b = (jnp.dot(tri, p0, preferred_element_type=F32) + jnp.dot(tri, p1, preferred_element_type=F32)
             + jnp.dot(tri, p2, preferred_element_type=F32))
        return q, k, f, b

    def pair_scores(q, k, f, b):
        qb, kb = q.astype(BF16), k.astype(BF16)
        lhs, rhs = [qb], [kb]
        for m in SCAN_LEVELS:
            if m == 1:
                lhs.append((q * f).astype(BF16))
                rhs.append(kb)
            else:
                d = jnp.exp2(-jnp.abs(b - _mid_rows(b, m, reverse)))
                lhs.append((q * d).astype(BF16))
                rhs.append((k * d).astype(BF16))
        zero = jnp.zeros_like(kb)
        lhs.append(qb)
        rhs.append(zero)
        total = None
        for j in range(N_SCORE_TILES):
            a = jnp.concatenate([lhs[2 * j], lhs[2 * j + 1]], axis=1)
            w = jnp.concatenate([jnp.concatenate([rhs[2 * j], zero], axis=1),
                                 jnp.concatenate([zero, rhs[2 * j + 1]], axis=1)], axis=0)
            s = mask_ref[j] * lax.dot_general(a, w, _NT, preferred_element_type=F32)
            total = s if total is None else total + s
        return total.astype(BF16)

    def outputs(cc, q, k, b, scores):
        sl = rows_of(cc)
        v = v_ref[sl, :].astype(BF16)
        b_end = b[last:last + 1]
        v_rep = jnp.concatenate([v, v], axis=0)
        o_ref[sl, :] = jnp.dot(scores, v_rep, preferred_element_type=F32)
        qin_ref[sl, :] = (q * jnp.exp2(b)).astype(BF16)
        k_in = (k * jnp.exp2(b_end - b)).astype(BF16)
        u_ref[cc] = lax.dot_general(v, k_in, _TN, preferred_element_type=F32)
        dec_ref[cc] = jnp.broadcast_to(jnp.exp2(b_end), (SUBLANES, HG_D))

    def intra(ccs):
        staged = [gates(cc) for cc in ccs]
        scores = [pair_scores(*s) for s in staged]
        for cc, (q, k, _, b), s in zip(ccs, staged, scores):
            outputs(cc, q, k, b, s)

    def inter(ccs):
        for cc in ccs:
            sl = rows_of(cc)
            st = st_ref[...]
            o_ref[sl, :] += lax.dot_general(qin_ref[sl, :], st.astype(BF16), _NT, preferred_element_type=F32)
            st_ref[...] = st * dec_ref[cc][0:1] + u_ref[cc]

    def over_chunks(body, group):
        def step(i, carry):
            cis = [i * group + u for u in range(group)]
            body([n_chunks - 1 - ci if reverse else ci for ci in cis])
            return carry
        lax.fori_loop(0, n_chunks // group, step, 0)

    over_chunks(intra, unroll_intra)
    over_chunks(inter, unroll_inter)


def hgrn_scan(proj, lb, lay, reverse, rows=2048, unroll_intra=8, unroll_inter=32):
    t = proj.shape[0]
    rows = _tile(lay.l_min, rows)
    nb = t // rows
    n_chunks = rows // HG_CHUNK
    unroll_intra, unroll_inter = _tile(n_chunks, unroll_intra), _tile(n_chunks, unroll_inter)
    tri, masks = _hgrn_constants(reverse)
    z_part = 2 if reverse else 1

    def rowblk(b):
        return nb - 1 - b if reverse else b

    def col(part):
        return lambda h, b: (rowblk(b), part * HG_HEADS + h)

    kern = functools.partial(_hgrn_scan_kernel, lay=lay, rows=rows, reverse=reverse,
                             unroll_intra=unroll_intra, unroll_inter=unroll_inter)
    return pl.pallas_call(
        kern,
        grid=(HG_HEADS, nb),
        in_specs=[pl.BlockSpec((rows, HG_D), col(0)),
                  pl.BlockSpec((rows, HG_D), col(z_part)),
                  pl.BlockSpec((rows, HG_D), col(3)),
                  pl.BlockSpec((1, HG_D), lambda h, b: (0, h)),
                  pl.BlockSpec(tri.shape, lambda h, b: (0, 0)),
                  pl.BlockSpec(masks.shape, lambda h, b: (0, 0, 0))],
        out_specs=pl.BlockSpec((rows, HG_D), lambda h, b: (rowblk(b), h)),
        out_shape=jax.ShapeDtypeStruct((t, D_MODEL), F32),
        scratch_shapes=[pltpu.VMEM((HG_D, HG_D), F32),
                        pltpu.VMEM((rows, HG_D), BF16),
                        pltpu.VMEM((n_chunks, HG_D, HG_D), F32),
                        pltpu.VMEM((n_chunks, SUBLANES, HG_D), F32)],
        compiler_params=_params(2),
        name="hgrn_scan_bwd" if reverse else "hgrn_scan_fwd",
    )(proj, proj, proj, lb.reshape(1, D_MODEL), jnp.asarray(tri, BF16), jnp.asarray(masks, F32))


def _hgrn_out_kernel(ofw_ref, obw_ref, gate_ref, ng_ref, w_ref, x_ref, pg_ref, out_ref, y_ref):
    ng = ng_ref[...]
    for h in range(HG_HEADS):
        sl = slice(h * HG_D, (h + 1) * HG_D)
        o = _rms(ofw_ref[:, sl] + obw_ref[:, sl], ng)
        gate = gate_ref[:, sl]
        y_ref[:, sl] = (o * (gate * jax.nn.sigmoid(gate))).astype(BF16)
    m = jnp.dot(y_ref[...], w_ref[...], preferred_element_type=F32)
    out_ref[...] = x_ref[...] + _rms(m, pg_ref[...])


def hgrn_out(o_fw, o_bw, proj, norm_g, w_out, layer, x, post_g, tm=512):
    t, d = x.shape
    tm = _tile(t, tm)
    row = lambda i: (i, 0)
    const = lambda i: (0, 0)
    return pl.pallas_call(
        _hgrn_out_kernel,
        grid=(t // tm,),
        in_specs=[pl.BlockSpec((tm, d), row),
                  pl.BlockSpec((tm, d), row),
                  pl.BlockSpec((tm, d), lambda i: (i, 4)),
                  pl.BlockSpec((1, HG_D), const),
                  pl.BlockSpec((None, d, d), lambda i: (layer, 0, 0), pipeline_mode=pl.Buffered(1)),
                  pl.BlockSpec((tm, d), row),
                  pl.BlockSpec((1, d), const)],
        out_specs=pl.BlockSpec((tm, d), row),
        out_shape=jax.ShapeDtypeStruct((t, d), F32),
        scratch_shapes=[pltpu.VMEM((tm, d), BF16)],
        compiler_params=_params(1),
        name="hgrn_out",
    )(o_fw, o_bw, proj, norm_g.reshape(1, HG_D), w_out, x, post_g.reshape(1, d))


def _out_proj_kernel(yp_ref, ys_ref, w_ref, x_ref, pg_ref, out_ref, *, prompt_tiles):
    def project(y_ref):
        m = jnp.dot(y_ref[...], w_ref[...], preferred_element_type=F32)
        out_ref[...] = x_ref[...] + _rms(m, pg_ref[...])

    pl.when(pl.program_id(0) < prompt_tiles)(lambda: project(yp_ref))
    pl.when(pl.program_id(0) >= prompt_tiles)(lambda: project(ys_ref))


def out_proj(y_prompt, y_sample, w_out, layer, x, post_g, lay, tm=512):
    t, d = x.shape
    tm = _tile(lay.l_min, tm)
    prompt_tiles = lay.t_prompt // tm
    row = lambda i: (i, 0)
    const = lambda i: (0, 0)
    return pl.pallas_call(
        functools.partial(_out_proj_kernel, prompt_tiles=prompt_tiles),
        grid=(t // tm,),
        in_specs=[pl.BlockSpec((tm, d), lambda i: (jnp.minimum(i, prompt_tiles - 1), 0)),
                  pl.BlockSpec((tm, d), lambda i: (jnp.maximum(i - prompt_tiles, 0), 0)),
                  pl.BlockSpec((None, d, d), lambda i: (layer, 0, 0)),
                  pl.BlockSpec((tm, d), row),
                  pl.BlockSpec((1, d), const)],
        out_specs=pl.BlockSpec((tm, d), row),
        out_shape=jax.ShapeDtypeStruct((t, d), F32),
        compiler_params=_params(1),
        name="attn_out",
    )(y_prompt, y_sample, w_out, x, post_g.reshape(1, d))


ATTN_SUB_TILES = 2


def _diff_attn_kernel(q_ref, k_ref, v_ref, lam_ref, sg_ref, o_ref, *, lambda_init):
    lp = lam_ref[...]
    lam = (jnp.exp(jnp.sum(lp[0:1] * lp[1:2], axis=-1, keepdims=True))
           - jnp.exp(jnp.sum(lp[2:3] * lp[3:4], axis=-1, keepdims=True)) + lambda_init)
    v = v_ref[...]
    tq = q_ref.shape[0] // ATTN_SUB_TILES
    scores = [lax.dot_general(q_ref[t * tq:(t + 1) * tq, sl], k_ref[:, sl], _NT,
                              preferred_element_type=F32)
              for t in range(ATTN_SUB_TILES) for sl in (slice(0, DA_DH), slice(DA_DH, 2 * DA_DH))]
    outs = []
    for s in scores:
        p = jnp.exp2(s - jnp.max(s, axis=-1, keepdims=True))
        denom = jnp.sum(p, axis=-1, keepdims=True)
        outs.append(jnp.dot(p.astype(BF16), v, preferred_element_type=F32) / denom)
    for t in range(ATTN_SUB_TILES):
        o = outs[2 * t] - lam * outs[2 * t + 1]
        o_ref[t * tq:(t + 1) * tq, :] = (_rms(o, sg_ref[...]) * (1.0 - lambda_init)).astype(o_ref.dtype)


def diff_attention(qkv, lam_params, subln_g, lay, lambda_init):
    outs = []
    for row_off, n_seq, length in ((0, lay.n_prompt, lay.l_prompt),
                                   (lay.t_prompt, lay.n_sample, lay.l_sample)):
        tq = ATTN_SUB_TILES * _tile(length // ATTN_SUB_TILES, 512 if length <= 2048 else 256)
        nq = length // tq
        qoff, koff = row_off // tq, row_off // length
        outs.append(pl.pallas_call(
            functools.partial(_diff_attn_kernel, lambda_init=lambda_init),
            grid=(n_seq, DA_HEADS, nq),
            in_specs=[pl.BlockSpec((tq, DA_DV), lambda b, h, i, qoff=qoff, nq=nq: (qoff + b * nq + i, h)),
                      pl.BlockSpec((length, DA_DV), lambda b, h, i, koff=koff: (koff + b, DA_HEADS + h)),
                      pl.BlockSpec((length, DA_DV), lambda b, h, i, koff=koff: (koff + b, 2 * DA_HEADS + h)),
                      pl.BlockSpec((4, DA_DH), lambda b, h, i: (0, 0)),
                      pl.BlockSpec((1, DA_DV), lambda b, h, i: (0, 0))],
            out_specs=pl.BlockSpec((tq, DA_DV), lambda b, h, i, nq=nq: (b * nq + i, h)),
            out_shape=jax.ShapeDtypeStruct((n_seq * length, D_MODEL), BF16),
            compiler_params=_params(3),
            name="diff_attn_len%d" % length,
        )(qkv, qkv, qkv, lam_params, subln_g.reshape(1, DA_DV)))
    return outs


def _gelu_tanh(x):
    c = math.sqrt(2.0 / math.pi)
    half = 0.5 * x
    return half + half * jnp.tanh(x * (c + (0.044715 * c) * (x * x)))


def _ffn_kernel(x_ref, xp_ref, xn_ref, g_ref, wg_ref, wv_ref, cwg_ref, cwv_ref, cbg_ref, cbv_ref,
                wd_ref, pg_ref, out_ref, h_ref, ug_ref, uv_ref, *, lay, tm):
    i, j = pl.program_id(0), pl.program_id(1)
    halo = SUBLANES

    @pl.when(j == 0)
    def _():
        g = g_ref[...]
        row0 = i * tm
        length = lay.seq_len(row0)
        has_prev = (row0 % length) != 0
        has_next = ((row0 + tm) % length) != 0
        h_ref[halo:halo + tm, :] = _rms(x_ref[...], g).astype(BF16)
        h_ref[0:halo, :] = jnp.where(has_prev, _rms(xp_ref[...], g), 0.0).astype(BF16)
        h_ref[halo + tm:2 * halo + tm, :] = jnp.where(has_next, _rms(xn_ref[...], g), 0.0).astype(BF16)
        out_ref[...] = jnp.zeros_like(out_ref)

    h = h_ref[...]
    ug_ref[...] = jnp.dot(h, wg_ref[...], preferred_element_type=F32)
    uv_ref[...] = jnp.dot(h, wv_ref[...], preferred_element_type=F32)

    def conv(u_ref, cw_ref, cb_ref):
        return (u_ref[halo - 1:halo - 1 + tm, :] * cw_ref[0:1, :] + u_ref[halo:halo + tm, :] * cw_ref[1:2, :]
                + u_ref[halo + 1:halo + 1 + tm, :] * cw_ref[2:3, :] + cb_ref[...])

    act = _gelu_tanh(conv(ug_ref, cwg_ref, cbg_ref)) * conv(uv_ref, cwv_ref, cbv_ref)
    out_ref[...] += jnp.dot(act.astype(BF16), wd_ref[...], preferred_element_type=F32)

    @pl.when(j == pl.num_programs(1) - 1)
    def _():
        out_ref[...] = x_ref[...] + _rms(out_ref[...], pg_ref[...])


def conv_ffn(x, pre_g, w_up, conv_w, conv_b, w_down, layer, post_g, lay, tm=1024, tn=512):
    t, d = x.shape
    f = w_down.shape[1]
    tm, tn = _tile(lay.l_min, tm), _tile(f, tn)
    nf = f // tn
    hb = tm // SUBLANES
    last_hb = t // SUBLANES - 1
    kern = functools.partial(_ffn_kernel, lay=lay, tm=tm)
    return pl.pallas_call(
        kern,
        grid=(t // tm, nf),
        in_specs=[pl.BlockSpec((tm, d), lambda i, j: (i, 0), pipeline_mode=pl.Buffered(1)),
                  pl.BlockSpec((SUBLANES, d), lambda i, j: (jnp.maximum(i * hb - 1, 0), 0)),
                  pl.BlockSpec((SUBLANES, d), lambda i, j: (jnp.minimum((i + 1) * hb, last_hb), 0)),
                  pl.BlockSpec((1, d), lambda i, j: (0, 0)),
                  pl.BlockSpec((None, d, tn), lambda i, j: (layer, 0, j)),
                  pl.BlockSpec((None, d, tn), lambda i, j: (layer, 0, nf + j)),
                  pl.BlockSpec((None, 3, tn), lambda i, j: (layer, 0, j)),
                  pl.BlockSpec((None, 3, tn), lambda i, j: (layer, 0, nf + j)),
                  pl.BlockSpec((None, 1, tn), lambda i, j: (layer, 0, j)),
                  pl.BlockSpec((None, 1, tn), lambda i, j: (layer, 0, nf + j)),
                  pl.BlockSpec((None, tn, d), lambda i, j: (layer, j, 0)),
                  pl.BlockSpec((1, d), lambda i, j: (0, 0))],
        out_specs=pl.BlockSpec((tm, d), lambda i, j: (i, 0)),
        out_shape=jax.ShapeDtypeStruct((t, d), F32),
        scratch_shapes=[pltpu.VMEM((tm + 2 * SUBLANES, d), BF16),
                        pltpu.VMEM((tm + 2 * SUBLANES, tn), F32),
                        pltpu.VMEM((tm + 2 * SUBLANES, tn), F32)],
        compiler_params=_params(2),
        name="conv_ffn",
    )(x, x, x, pre_g.reshape(1, d), w_up, w_up, conv_w, conv_w, conv_b, conv_b, w_down, post_g.reshape(1, d))


def kernel(x_prompt, x_sample, pre_mix_g, post_mix_g, pre_ffn_g, post_ffn_g, hg_w_in, hg_w_out, hg_norm_g,
           hg_lower_bounds, da_w_qkv, da_w_out, da_lambda, da_subln_g, ffn_w_up, ffn_conv_w, ffn_conv_b,
           ffn_w_down):
    lay = Layout(x_prompt.shape[0], x_prompt.shape[1], x_sample.shape[0], x_sample.shape[1])
    x = jnp.concatenate([x_prompt.reshape(lay.t_prompt, D_MODEL), x_sample.reshape(lay.t_sample, D_MODEL)])
    rope_tab = rope_table(lay)
    sm = jax.nn.softmax(hg_lower_bounds.astype(F32), axis=0)
    lbs = jnp.cumsum(sm, axis=0) - sm[0:1]
    hg_w_in, hg_w_out, da_w_qkv, da_w_out, ffn_w_up, ffn_w_down = (
        w.astype(BF16) for w in (hg_w_in, hg_w_out, da_w_qkv, da_w_out, ffn_w_up, ffn_w_down))
    ffn_conv_b = ffn_conv_b.reshape(DEPTH, 1, -1)

    for i in range(DEPTH):
        j = i // 2
        if i % 2 == 0:
            proj = norm_matmul(x, pre_mix_g[i], hg_w_in, j, F32)
            o_fw = hgrn_scan(proj, lbs[i], lay, reverse=False)
            o_bw = hgrn_scan(proj, lbs[i], lay, reverse=True)
            x = hgrn_out(o_fw, o_bw, proj, hg_norm_g[j], hg_w_out, j, x, post_mix_g[i])
        else:
            lambda_init = 0.8 - 0.6 * math.exp(-0.3 * i)
            qkv = qkv_projection(x, pre_mix_g[i], da_w_qkv, j, rope_tab, lay)
            heads_prompt, heads_sample = diff_attention(qkv, da_lambda[j], da_subln_g[j], lay, lambda_init)
            x = out_proj(heads_prompt, heads_sample, da_w_out, j, x, post_mix_g[i], lay)
        x = conv_ffn(x, pre_ffn_g[i], ffn_w_up, ffn_conv_w, ffn_conv_b, ffn_w_down, i, post_ffn_g[i], lay)

    y_prompt = x[:lay.t_prompt].reshape(x_prompt.shape)
    y_sample = x[lay.t_prompt:].reshape(x_sample.shape)
    return (y_prompt, y_sample)
```

```python
import functools
import math

import numpy as np
import jax
import jax.numpy as jnp
from jax import lax
from jax.experimental import pallas as pl
from jax.experimental.pallas import tpu as pltpu

D_MODEL = 2048
DEPTH = 4
HG_HEADS = 16
HG_D = 128
HG_CHUNK = 64
DA_HEADS = 8
DA_DH = 128
DA_DV = 256
ROPE_DIM = 32
ROPE_THETA = 500000.0
EPS = 1e-6
F_FLOOR = 1e-30
LOG2_E = 1.0 / math.log(2.0)

MXU_DEPTH = 256
LANES = 128
SUBLANES = 8
VMEM_LIMIT_BYTES = 58 * 1024 * 1024

F32 = jnp.float32
BF16 = jnp.bfloat16

_NT = (((1,), (1,)), ((), ()))
_TN = (((0,), (0,)), ((), ()))


def _params(n_axes):
    return pltpu.CompilerParams(dimension_semantics=("arbitrary",) * n_axes,
                                vmem_limit_bytes=VMEM_LIMIT_BYTES)


def _rms(x, g):
    return x * lax.rsqrt(jnp.mean(x * x, axis=-1, keepdims=True) + EPS) * g


def _tile(n, pref):
    t = min(n, pref)
    while n % t:
        t //= 2
    return t


class Layout:
    def __init__(self, n_prompt, l_prompt, n_sample, l_sample):
        self.t_prompt = n_prompt * l_prompt
        self.l_prompt = l_prompt
        self.n_prompt = n_prompt
        self.t_sample = n_sample * l_sample
        self.l_sample = l_sample
        self.n_sample = n_sample
        self.t = self.t_prompt + self.t_sample
        self.l_min = min(l_prompt, l_sample)

    def seq_len(self, row):
        return jnp.where(row < self.t_prompt, self.l_prompt, self.l_sample)


def _norm_matmul_kernel(x_ref, g_ref, w_ref, o_ref, h_ref):
    @pl.when(pl.program_id(1) == 0)
    def _():
        h_ref[...] = _rms(x_ref[...], g_ref[...]).astype(BF16)

    o_ref[...] = jnp.dot(h_ref[...], w_ref[...], preferred_element_type=F32).astype(o_ref.dtype)


def norm_matmul(x, g, w, layer, out_dtype, tm=1024, tn=1024):
    t, d = x.shape
    n = w.shape[2]
    tm, tn = _tile(t, tm), _tile(n, tn)
    return pl.pallas_call(
        _norm_matmul_kernel,
        grid=(t // tm, n // tn),
        in_specs=[pl.BlockSpec((tm, d), lambda i, j: (i, 0)),
                  pl.BlockSpec((1, d), lambda i, j: (0, 0)),
                  pl.BlockSpec((None, d, tn), lambda i, j: (layer, 0, j))],
        out_specs=pl.BlockSpec((tm, tn), lambda i, j: (i, j)),
        out_shape=jax.ShapeDtypeStruct((t, n), out_dtype),
        scratch_shapes=[pltpu.VMEM((tm, d), BF16)],
        compiler_params=_params(2),
        name="norm_matmul",
    )(x, g.reshape(1, d), w)


def _qkv_kernel(x_ref, g_ref, w_ref, rope_ref, o_ref, h_ref):
    @pl.when(pl.program_id(1) == 0)
    def _():
        h_ref[...] = _rms(x_ref[...], g_ref[...]).astype(BF16)

    acc = jnp.dot(h_ref[...], w_ref[...], preferred_element_type=F32)
    cos = rope_ref[:, 0:LANES]
    sin_lo = rope_ref[:, LANES:2 * LANES]
    sin_hi = rope_ref[:, 2 * LANES:3 * LANES]
    half = ROPE_DIM // 2
    for c in range(acc.shape[1] // LANES):
        a = acc[:, c * LANES:(c + 1) * LANES]
        r = a * cos + pltpu.roll(a, LANES - half, 1) * sin_lo + pltpu.roll(a, half, 1) * sin_hi
        o_ref[:, c * LANES:(c + 1) * LANES] = r.astype(o_ref.dtype)


def qkv_projection(x, g, w, layer, rope_tab, lay, tm=1024, tn=1024):
    t, d = x.shape
    n = w.shape[2]
    tm, tn = _tile(lay.l_min, tm), _tile(D_MODEL, tn)
    tiles_per_part = D_MODEL // tn
    prompt_tiles = lay.t_prompt // tm

    def pos_block(i):
        return jnp.where(i < prompt_tiles, i % (lay.l_prompt // tm), (i - prompt_tiles) % (lay.l_sample // tm))

    return pl.pallas_call(
        _qkv_kernel,
        grid=(t // tm, n // tn),
        in_specs=[pl.BlockSpec((tm, d), lambda i, j: (i, 0)),
                  pl.BlockSpec((1, d), lambda i, j: (0, 0)),
                  pl.BlockSpec((None, d, tn), lambda i, j: (layer, 0, j)),
                  pl.BlockSpec((None, tm, 3 * LANES), lambda i, j: (j // tiles_per_part, pos_block(i), 0))],
        out_specs=pl.BlockSpec((tm, tn), lambda i, j: (i, j)),
        out_shape=jax.ShapeDtypeStruct((t, n), BF16),
        scratch_shapes=[pltpu.VMEM((tm, d), BF16)],
        compiler_params=_params(2),
        name="qkv_rope",
    )(x, g.reshape(1, d), w, rope_tab)


def rope_table(lay):
    half = ROPE_DIM // 2

    def tab(length):
        inv = 1.0 / (ROPE_THETA ** (jnp.arange(0, ROPE_DIM, 2, dtype=F32) / ROPE_DIM))
        ang = jnp.arange(length, dtype=F32)[:, None] * inv[None, :]
        cos, sin = jnp.cos(ang), jnp.sin(ang)
        ones = jnp.ones((length, LANES - ROPE_DIM), F32)
        zeros = jnp.zeros((length, LANES - half), F32)
        c = jnp.concatenate([cos, cos, ones], axis=1)
        s_lo = jnp.concatenate([-sin, zeros], axis=1)
        s_hi = jnp.concatenate([jnp.zeros((length, half), F32), sin, zeros[:, :LANES - ROPE_DIM]], axis=1)
        return jnp.concatenate([c, s_lo, s_hi], axis=1)

    l_max = max(lay.l_prompt, lay.l_sample)
    k_tab = tab(l_max)
    identity = jnp.concatenate([jnp.ones((l_max, LANES), F32), jnp.zeros((l_max, 2 * LANES), F32)], axis=1)
    return jnp.stack([k_tab * (DA_DH ** -0.5 * LOG2_E), k_tab, identity])


SCAN_LEVELS = tuple(HG_CHUNK >> s for s in range(1, int(math.log2(HG_CHUNK)) + 1))
N_SCORE_TILES = (len(SCAN_LEVELS) + 2) // 2


def _hgrn_constants(reverse):
    c = HG_CHUNK
    t = np.arange(c)[:, None]
    s = np.arange(c)[None, :]
    masks = [t == s]
    for m in SCAN_LEVELS:
        masks.append((t // (2 * m) == s // (2 * m)) & ((t % (2 * m)) >= m) & ((s % (2 * m)) < m))
    masks.append(np.zeros_like(masks[0]))
    assert len(masks) == 2 * N_SCORE_TILES
    tri = (s <= t).astype(np.float32)
    masks = np.stack(masks).astype(np.float32)
    if reverse:
        tri, masks = tri[::-1, ::-1], masks[:, ::-1, ::-1]
    masks = np.concatenate([masks[0::2], masks[1::2]], axis=2)
    return tri, masks


def _mid_rows(b, m, reverse):
    c, d = b.shape
    r = m if reverse else m - 1
    if 2 * m >= SUBLANES:
        b3 = b.reshape(c // (2 * m), 2 * m, d)
        return jnp.broadcast_to(b3[:, r:r + 1, :], b3.shape).reshape(c, d)
    assert 4 * m == SUBLANES
    b3 = b.reshape(c // SUBLANES, SUBLANES, d)
    sub = lax.broadcasted_iota(jnp.int32, b3.shape, 1)
    first = jnp.broadcast_to(b3[:, r:r + 1, :], b3.shape)
    second = jnp.broadcast_to(b3[:, 2 * m + r:2 * m + r + 1, :], b3.shape)
    return jnp.where(sub < 2 * m, first, second).reshape(c, d)


def _hgrn_scan_kernel(q_ref, z_ref, v_ref, lb_ref, tri_ref, mask_ref, o_ref, st_ref, qin_ref, u_ref, dec_ref,
                      *, lay, rows, reverse, unroll_intra, unroll_inter):
    c = HG_CHUNK
    n_chunks = rows // c
    blk = pl.program_id(1)
    if reverse:
        blk = pl.num_programs(1) - 1 - blk
    row0 = blk * rows
    first_row = row0 + rows if reverse else row0

    @pl.when(first_row % lay.seq_len(row0) == 0)
    def _():
        st_ref[...] = jnp.zeros_like(st_ref)

    lb = lb_ref[...]
    one_m_lb = 1.0 - lb
    last = 0 if reverse else c - 1

    def rows_of(cc):
        return pl.ds(pl.multiple_of(cc * c, c), c)

    def gates(cc):
        sl = rows_of(cc)
        z = z_ref[sl, :]
        q = q_ref[sl, :]
        e = jnp.exp2(jnp.abs(z) * -LOG2_E)
        r = 1.0 / (1.0 + e)
        er = e * r
        f = jnp.maximum(lb + one_m_lb * jnp.where(z >= 0, r, er), F_FLOOR)
        k = one_m_lb * jnp.where(z >= 0, er, r)
        q = q / (1.0 + jnp.exp2(q * -LOG2_E))
        logf = jnp.log(f) * LOG2_E
        p0 = logf.astype(BF16)
        r1 = logf - p0.astype(F32)
        p1 = r1.astype(BF16)
        p2 = (r1 - p1.astype(F32)).astype(BF16)
        tri = tri_ref[...]
        b = (jnp.dot(tri, p0, preferred_element_type=F32) + jnp.dot(tri, p1, preferred_element_type=F32)
             + jnp.dot(tri, p2, preferred_element_type=F32))
        return q, k, f, b

    def pair_scores(q, k, f, b):
        qb, kb = q.astype(BF16), k.astype(BF16)
        lhs, rhs = [qb], [kb]
        for m in SCAN_LEVELS:
            if m == 1:
                lhs.append((q * f).astype(BF16))
                rhs.append(kb)
            else:
                d = jnp.exp2(-jnp.abs(b - _mid_rows(b, m, reverse)))
                lhs.append((q * d).astype(BF16))
                rhs.append((k * d).astype(BF16))
        zero = jnp.zeros_like(kb)
        lhs.append(qb)
        rhs.append(zero)
        total = None
        for j in range(N_SCORE_TILES):
            a = jnp.concatenate([lhs[2 * j], lhs[2 * j + 1]], axis=1)
            w = jnp.concatenate([jnp.concatenate([rhs[2 * j], zero], axis=1),
                                 jnp.concatenate([zero, rhs[2 * j + 1]], axis=1)], axis=0)
            s = mask_ref[j] * lax.dot_general(a, w, _NT, preferred_element_type=F32)
            total = s if total is None else total + s
        return total.astype(BF16)

    def outputs(cc, q, k, b, scores):
        sl = rows_of(cc)
        v = v_ref[sl, :].astype(BF16)
        b_end = b[last:last + 1]
        v_rep = jnp.concatenate([v, v], axis=0)
        o_ref[sl, :] = jnp.dot(scores, v_rep, preferred_element_type=F32)
        qin_ref[sl, :] = (q * jnp.exp2(b)).astype(BF16)
        k_in = (k * jnp.exp2(b_end - b)).astype(BF16)
        u_ref[cc] = lax.dot_general(v, k_in, _TN, preferred_element_type=F32)
        dec_ref[cc] = jnp.broadcast_to(jnp.exp2(b_end), (SUBLANES, HG_D))

    def intra(ccs):
        staged = [gates(cc) for cc in ccs]
        scores = [pair_scores(*s) for s in staged]
        for cc, (q, k, _, b), s in zip(ccs, staged, scores):
            outputs(cc, q, k, b, s)

    def inter(ccs):
        for cc in ccs:
            sl = rows_of(cc)
            st = st_ref[...]
            o_ref[sl, :] += lax.dot_general(qin_ref[sl, :], st.astype(BF16), _NT, preferred_element_type=F32)
            st_ref[...] = st * dec_ref[cc][0:1] + u_ref[cc]

    def over_chunks(body, group):
        def step(i, carry):
            cis = [i * group + u for u in range(group)]
            body([n_chunks - 1 - ci if reverse else ci for ci in cis])
            return carry
        lax.fori_loop(0, n_chunks // group, step, 0)

    over_chunks(intra, unroll_intra)
    over_chunks(inter, unroll_inter)


def hgrn_scan(proj, lb, lay, reverse, rows=2048, unroll_intra=8, unroll_inter=32):
    t = proj.shape[0]
    rows = _tile(lay.l_min, rows)
    nb = t // rows
    n_chunks = rows // HG_CHUNK
    unroll_intra, unroll_inter = _tile(n_chunks, unroll_intra), _tile(n_chunks, unroll_inter)
    tri, masks = _hgrn_constants(reverse)
    z_part = 2 if reverse else 1

    def rowblk(b):
        return nb - 1 - b if reverse else b

    def col(part):
        return lambda h, b: (rowblk(b), part * HG_HEADS + h)

    kern = functools.partial(_hgrn_scan_kernel, lay=lay, rows=rows, reverse=reverse,
                             unroll_intra=unroll_intra, unroll_inter=unroll_inter)
    return pl.pallas_call(
        kern,
        grid=(HG_HEADS, nb),
        in_specs=[pl.BlockSpec((rows, HG_D), col(0)),
                  pl.BlockSpec((rows, HG_D), col(z_part)),
                  pl.BlockSpec((rows, HG_D), col(3)),
                  pl.BlockSpec((1, HG_D), lambda h, b: (0, h)),
                  pl.BlockSpec(tri.shape, lambda h, b: (0, 0)),
                  pl.BlockSpec(masks.shape, lambda h, b: (0, 0, 0))],
        out_specs=pl.BlockSpec((rows, HG_D), lambda h, b: (rowblk(b), h)),
        out_shape=jax.ShapeDtypeStruct((t, D_MODEL), F32),
        scratch_shapes=[pltpu.VMEM((HG_D, HG_D), F32),
                        pltpu.VMEM((rows, HG_D), BF16),
                        pltpu.VMEM((n_chunks, HG_D, HG_D), F32),
                        pltpu.VMEM((n_chunks, SUBLANES, HG_D), F32)],
        compiler_params=_params(2),
        name="hgrn_scan_bwd" if reverse else "hgrn_scan_fwd",
    )(proj, proj, proj, lb.reshape(1, D_MODEL), jnp.asarray(tri, BF16), jnp.asarray(masks, F32))


def _hgrn_out_kernel(ofw_ref, obw_ref, gate_ref, ng_ref, w_ref, x_ref, pg_ref, out_ref, y_ref):
    ng = ng_ref[...]
    for h in range(HG_HEADS):
        sl = slice(h * HG_D, (h + 1) * HG_D)
        o = _rms(ofw_ref[:, sl] + obw_ref[:, sl], ng)
        gate = gate_ref[:, sl]
        y_ref[:, sl] = (o * (gate * jax.nn.sigmoid(gate))).astype(BF16)
    m = jnp.dot(y_ref[...], w_ref[...], preferred_element_type=F32)
    out_ref[...] = x_ref[...] + _rms(m, pg_ref[...])


def hgrn_out(o_fw, o_bw, proj, norm_g, w_out, layer, x, post_g, tm=512):
    t, d = x.shape
    tm = _tile(t, tm)
    row = lambda i: (i, 0)
    const = lambda i: (0, 0)
    return pl.pallas_call(
        _hgrn_out_kernel,
        grid=(t // tm,),
        in_specs=[pl.BlockSpec((tm, d), row),
                  pl.BlockSpec((tm, d), row),
                  pl.BlockSpec((tm, d), lambda i: (i, 4)),
                  pl.BlockSpec((1, HG_D), const),
                  pl.BlockSpec((None, d, d), lambda i: (layer, 0, 0), pipeline_mode=pl.Buffered(1)),
                  pl.BlockSpec((tm, d), row),
                  pl.BlockSpec((1, d), const)],
        out_specs=pl.BlockSpec((tm, d), row),
        out_shape=jax.ShapeDtypeStruct((t, d), F32),
        scratch_shapes=[pltpu.VMEM((tm, d), BF16)],
        compiler_params=_params(1),
        name="hgrn_out",
    )(o_fw, o_bw, proj, norm_g.reshape(1, HG_D), w_out, x, post_g.reshape(1, d))


def _out_proj_kernel(yp_ref, ys_ref, w_ref, x_ref, pg_ref, out_ref, *, prompt_tiles):
    def project(y_ref):
        m = jnp.dot(y_ref[...], w_ref[...], preferred_element_type=F32)
        out_ref[...] = x_ref[...] + _rms(m, pg_ref[...])

    pl.when(pl.program_id(0) < prompt_tiles)(lambda: project(yp_ref))
    pl.when(pl.program_id(0) >= prompt_tiles)(lambda: project(ys_ref))


def out_proj(y_prompt, y_sample, w_out, layer, x, post_g, lay, tm=512):
    t, d = x.shape
    tm = _tile(lay.l_min, tm)
    prompt_tiles = lay.t_prompt // tm
    row = lambda i: (i, 0)
    const = lambda i: (0, 0)
    return pl.pallas_call(
        functools.partial(_out_proj_kernel, prompt_tiles=prompt_tiles),
        grid=(t // tm,),
        in_specs=[pl.BlockSpec((tm, d), lambda i: (jnp.minimum(i, prompt_tiles - 1), 0)),
                  pl.BlockSpec((tm, d), lambda i: (jnp.maximum(i - prompt_tiles, 0), 0)),
                  pl.BlockSpec((None, d, d), lambda i: (layer, 0, 0)),
                  pl.BlockSpec((tm, d), row),
                  pl.BlockSpec((1, d), const)],
        out_specs=pl.BlockSpec((tm, d), row),
        out_shape=jax.ShapeDtypeStruct((t, d), F32),
        compiler_params=_params(1),
        name="attn_out",
    )(y_prompt, y_sample, w_out, x, post_g.reshape(1, d))


ATTN_SUB_TILES = 2
ATTN_KEY_CHUNK = MXU_DEPTH


def _diff_attn_kernel(q_ref, k_ref, v_ref, lam_ref, sg_ref, o_ref, *, lambda_init):
    lp = lam_ref[...]
    lam = (jnp.exp(jnp.sum(lp[0:1] * lp[1:2], axis=-1, keepdims=True))
           - jnp.exp(jnp.sum(lp[2:3] * lp[3:4], axis=-1, keepdims=True)) + lambda_init)
    v = v_ref[...]
    tq = q_ref.shape[0] // ATTN_SUB_TILES
    scores = [lax.dot_general(q_ref[t * tq:(t + 1) * tq, sl], k_ref[:, sl], _NT,
                              preferred_element_type=F32)
              for t in range(ATTN_SUB_TILES) for sl in (slice(0, DA_DH), slice(DA_DH, 2 * DA_DH))]
    outs = []
    n_keys = v.shape[0]
    for s in scores:
        m = jnp.max(s, axis=-1, keepdims=True)
        acc = denom = None
        for lo in range(0, n_keys, ATTN_KEY_CHUNK):
            p = jnp.exp2(s[:, lo:lo + ATTN_KEY_CHUNK] - m)
            part = jnp.dot(p.astype(BF16), v[lo:lo + ATTN_KEY_CHUNK], preferred_element_type=F32)
            row_sum = jnp.sum(p, axis=-1, keepdims=True)
            acc = part if acc is None else acc + part
            denom = row_sum if denom is None else denom + row_sum
        outs.append(acc / denom)
    for t in range(ATTN_SUB_TILES):
        o = outs[2 * t] - lam * outs[2 * t + 1]
        o_ref[t * tq:(t + 1) * tq, :] = (_rms(o, sg_ref[...]) * (1.0 - lambda_init)).astype(o_ref.dtype)


def diff_attention(qkv, lam_params, subln_g, lay, lambda_init):
    outs = []
    for row_off, n_seq, length in ((0, lay.n_prompt, lay.l_prompt),
                                   (lay.t_prompt, lay.n_sample, lay.l_sample)):
        tq = ATTN_SUB_TILES * _tile(length // ATTN_SUB_TILES, 512 if length <= 2048 else 256)
        nq = length // tq
        qoff, koff = row_off // tq, row_off // length
        outs.append(pl.pallas_call(
            functools.partial(_diff_attn_kernel, lambda_init=lambda_init),
            grid=(n_seq, DA_HEADS, nq),
            in_specs=[pl.BlockSpec((tq, DA_DV), lambda b, h, i, qoff=qoff, nq=nq: (qoff + b * nq + i, h)),
                      pl.BlockSpec((length, DA_DV), lambda b, h, i, koff=koff: (koff + b, DA_HEADS + h)),
                      pl.BlockSpec((length, DA_DV), lambda b, h, i, koff=koff: (koff + b, 2 * DA_HEADS + h)),
                      pl.BlockSpec((4, DA_DH), lambda b, h, i: (0, 0)),
                      pl.BlockSpec((1, DA_DV), lambda b, h, i: (0, 0))],
            out_specs=pl.BlockSpec((tq, DA_DV), lambda b, h, i, nq=nq: (b * nq + i, h)),
            out_shape=jax.ShapeDtypeStruct((n_seq * length, D_MODEL), BF16),
            compiler_params=_params(3),
            name="diff_attn_len%d" % length,
        )(qkv, qkv, qkv, lam_params, subln_g.reshape(1, DA_DV)))
    return outs


def _gelu_tanh(x):
    c = math.sqrt(2.0 / math.pi)
    half = 0.5 * x
    return half + half * jnp.tanh(x * (c + (0.044715 * c) * (x * x)))


def _ffn_kernel(x_ref, xp_ref, xn_ref, g_ref, wg_ref, wv_ref, cwg_ref, cwv_ref, cbg_ref, cbv_ref,
                wd_ref, pg_ref, out_ref, h_ref, ug_ref, uv_ref, *, lay, tm):
    i, j = pl.program_id(0), pl.program_id(1)
    halo = SUBLANES

    @pl.when(j == 0)
    def _():
        g = g_ref[...]
        row0 = i * tm
        length = lay.seq_len(row0)
        has_prev = (row0 % length) != 0
        has_next = ((row0 + tm) % length) != 0
        h_ref[halo:halo + tm, :] = _rms(x_ref[...], g).astype(BF16)
        h_ref[0:halo, :] = jnp.where(has_prev, _rms(xp_ref[...], g), 0.0).astype(BF16)
        h_ref[halo + tm:2 * halo + tm, :] = jnp.where(has_next, _rms(xn_ref[...], g), 0.0).astype(BF16)
        out_ref[...] = jnp.zeros_like(out_ref)

    h = h_ref[...]
    ug_ref[...] = jnp.dot(h, wg_ref[...], preferred_element_type=F32)
    uv_ref[...] = jnp.dot(h, wv_ref[...], preferred_element_type=F32)

    def conv(u_ref, cw_ref, cb_ref):
        return (u_ref[halo - 1:halo - 1 + tm, :] * cw_ref[0:1, :] + u_ref[halo:halo + tm, :] * cw_ref[1:2, :]
                + u_ref[halo + 1:halo + 1 + tm, :] * cw_ref[2:3, :] + cb_ref[...])

    act = _gelu_tanh(conv(ug_ref, cwg_ref, cbg_ref)) * conv(uv_ref, cwv_ref, cbv_ref)
    out_ref[...] += jnp.dot(act.astype(BF16), wd_ref[...], preferred_element_type=F32)

    @pl.when(j == pl.num_programs(1) - 1)
    def _():
        out_ref[...] = x_ref[...] + _rms(out_ref[...], pg_ref[...])


def conv_ffn(x, pre_g, w_up, conv_w, conv_b, w_down, layer, post_g, lay, tm=1024, tn=512):
    t, d = x.shape
    f = w_down.shape[1]
    tm, tn = _tile(lay.l_min, tm), _tile(f, tn)
    nf = f // tn
    hb = tm // SUBLANES
    last_hb = t // SUBLANES - 1
    kern = functools.partial(_ffn_kernel, lay=lay, tm=tm)
    return pl.pallas_call(
        kern,
        grid=(t // tm, nf),
        in_specs=[pl.BlockSpec((tm, d), lambda i, j: (i, 0), pipeline_mode=pl.Buffered(1)),
                  pl.BlockSpec((SUBLANES, d), lambda i, j: (jnp.maximum(i * hb - 1, 0), 0)),
                  pl.BlockSpec((SUBLANES, d), lambda i, j: (jnp.minimum((i + 1) * hb, last_hb), 0)),
                  pl.BlockSpec((1, d), lambda i, j: (0, 0)),
                  pl.BlockSpec((None, d, tn), lambda i, j: (layer, 0, j)),
                  pl.BlockSpec((None, d, tn), lambda i, j: (layer, 0, nf + j)),
                  pl.BlockSpec((None, 3, tn), lambda i, j: (layer, 0, j)),
                  pl.BlockSpec((None, 3, tn), lambda i, j: (layer, 0, nf + j)),
                  pl.BlockSpec((None, 1, tn), lambda i, j: (layer, 0, j)),
                  pl.BlockSpec((None, 1, tn), lambda i, j: (layer, 0, nf + j)),
                  pl.BlockSpec((None, tn, d), lambda i, j: (layer, j, 0)),
                  pl.BlockSpec((1, d), lambda i, j: (0, 0))],
        out_specs=pl.BlockSpec((tm, d), lambda i, j: (i, 0)),
        out_shape=jax.ShapeDtypeStruct((t, d), F32),
        scratch_shapes=[pltpu.VMEM((tm + 2 * SUBLANES, d), BF16),
                        pltpu.VMEM((tm + 2 * SUBLANES, tn), F32),
                        pltpu.VMEM((tm + 2 * SUBLANES, tn), F32)],
        compiler_params=_params(2),
        name="conv_ffn",
    )(x, x, x, pre_g.reshape(1, d), w_up, w_up, conv_w, conv_w, conv_b, conv_b, w_down, post_g.reshape(1, d))


def kernel(x_prompt, x_sample, pre_mix_g, post_mix_g, pre_ffn_g, post_ffn_g, hg_w_in, hg_w_out, hg_norm_g,
           hg_lower_bounds, da_w_qkv, da_w_out, da_lambda, da_subln_g, ffn_w_up, ffn_conv_w, ffn_conv_b,
           ffn_w_down):
    lay = Layout(x_prompt.shape[0], x_prompt.shape[1], x_sample.shape[0], x_sample.shape[1])
    x = jnp.concatenate([x_prompt.reshape(lay.t_prompt, D_MODEL), x_sample.reshape(lay.t_sample, D_MODEL)])
    rope_tab = rope_table(lay)
    sm = jax.nn.softmax(hg_lower_bounds.astype(F32), axis=0)
    lbs = jnp.cumsum(sm, axis=0) - sm[0:1]
    hg_w_in, hg_w_out, da_w_qkv, da_w_out, ffn_w_up, ffn_w_down = (
        w.astype(BF16) for w in (hg_w_in, hg_w_out, da_w_qkv, da_w_out, ffn_w_up, ffn_w_down))
    ffn_conv_b = ffn_conv_b.reshape(DEPTH, 1, -1)

    for i in range(DEPTH):
        j = i // 2
        if i % 2 == 0:
            proj = norm_matmul(x, pre_mix_g[i], hg_w_in, j, F32)
            o_fw = hgrn_scan(proj, lbs[i], lay, reverse=False)
            o_bw = hgrn_scan(proj, lbs[i], lay, reverse=True)
            x = hgrn_out(o_fw, o_bw, proj, hg_norm_g[j], hg_w_out, j, x, post_mix_g[i])
        else:
            lambda_init = 0.8 - 0.6 * math.exp(-0.3 * i)
            qkv = qkv_projection(x, pre_mix_g[i], da_w_qkv, j, rope_tab, lay)
            heads_prompt, heads_sample = diff_attention(qkv, da_lambda[j], da_subln_g[j], lay, lambda_init)
            x = out_proj(heads_prompt, heads_sample, da_w_out, j, x, post_mix_g[i], lay)
        x = conv_ffn(x, pre_ffn_g[i], ffn_w_up, ffn_conv_w, ffn_conv_b, ffn_w_down, i, post_ffn_g[i], lay)

    y_prompt = x[:lay.t_prompt].reshape(x_prompt.shape)
    y_sample = x[lay.t_prompt:].reshape(x_sample.shape)
    return (y_prompt, y_sample)
```
